```python
import math
import jax, jax.numpy as jnp
from jax import lax
import numpy as np

D_MODEL = 1024
BATCH = 2
SEQ = 8192
DEPTH = 2

BRANCH_WIDTH = D_MODEL // 2
A_DK = 64
A_DV = 2 * A_DK
A_HEADS = BRANCH_WIDTH // A_DV
A_QK = A_HEADS * 2 * A_DK
B_DK = 64
B_DV = 64
B_HEADS = BRANCH_WIDTH // B_DV
B_GROUPS = 2
B_HPG = B_HEADS // B_GROUPS
CMP_LEN = 32
CMP_STRIDE = 16
SLC_LEN = 64
SLC_TOPK = 16
WIN = 512
PHI_HIDDEN = 256
N_NSA_BRANCH = 3
C_WIDTH = BRANCH_WIDTH
C_WINDOWS = (2, 4, 8, 16)
C_GROUPS = len(C_WINDOWS)
C_GDIM = C_WIDTH // C_GROUPS
N_BRANCH = 3

Q_BLOCK = 128
EPS = 1e-6
NEG_INF = -1e30
BIG = 1e30

SPLIT_SIZES = (A_QK, A_QK, BRANCH_WIDTH, BRANCH_WIDTH,
               B_HEADS * B_DK, N_NSA_BRANCH * 2 * B_GROUPS * B_DK,
               B_HEADS * N_NSA_BRANCH, BRANCH_WIDTH,
               C_WIDTH, C_WIDTH,
               N_BRANCH * D_MODEL)
N_IN = sum(SPLIT_SIZES)

kernel_name = 'hybrid_diff_nsa_pool_block'


def rms_norm(x, g):
    xf = x.astype(jnp.float32)
    y = xf * lax.rsqrt(jnp.mean(xf * xf, axis=-1, keepdims=True) + EPS)
    return (y * g.astype(jnp.float32)).astype(x.dtype)


def diff_attention(q, k, v, lam):
    Bn, H, _, S, dk = q.shape
    dv = v.shape[-1]
    scale = dk ** -0.5
    key_pos = jnp.arange(S)

    def block(i):
        q0 = i * Q_BLOCK
        qb = lax.dynamic_slice_in_dim(q, q0, Q_BLOCK, axis=3)
        sc = jnp.einsum('bhmqd,bhmkd->bhmqk', qb, k).astype(jnp.float32) * scale
        qpos = q0 + jnp.arange(Q_BLOCK)
        mask = key_pos[None, :] <= qpos[:, None]
        p = jax.nn.softmax(jnp.where(mask, sc, NEG_INF), axis=-1)
        w = p[:, :, 0] - lam * p[:, :, 1]
        return jnp.einsum('bhqk,bhkd->bhqd', w.astype(v.dtype), v)

    out = lax.map(block, jnp.arange(S // Q_BLOCK))
    return out.transpose(1, 2, 0, 3, 4).reshape(Bn, H, S, dv)


def compress_blocks(kv, pos, w1, w2):
    S, d = kv.shape[2], kv.shape[3]
    nc = (S - CMP_LEN) // CMP_STRIDE + 1
    idx = np.arange(nc)[:, None] * CMP_STRIDE + np.arange(CMP_LEN)[None, :]
    blocks = kv[:, :, idx] + pos
    flat = blocks.reshape(blocks.shape[0], blocks.shape[1], nc, CMP_LEN * d)
    return jax.nn.silu(flat @ w1) @ w2


def overlap_matrix(S):
    nc = (S - CMP_LEN) // CMP_STRIDE + 1
    ns = S // SLC_LEN
    cs = np.arange(nc) * CMP_STRIDE
    ce = cs + CMP_LEN
    ss = np.arange(ns) * SLC_LEN
    se = ss + SLC_LEN
    ov = np.clip(np.minimum(ce[:, None], se[None, :]) - np.maximum(cs[:, None], ss[None, :]), 0, None)
    return jnp.asarray(ov / CMP_LEN, dtype=jnp.float32)


def nsa_attention(q, k_cmp, v_cmp, k_slc, v_slc, k_win_pad, v_win_pad, gates):
    Bn, G, Hg, S, dk = q.shape
    nc = k_cmp.shape[2]
    ns = k_slc.shape[2]
    topk = min(SLC_TOPK, ns)
    scale = dk ** -0.5
    ov = overlap_matrix(S)
    cmp_end = jnp.arange(nc) * CMP_STRIDE + CMP_LEN - 1
    blk = jnp.arange(ns)
    bi = jnp.arange(Bn)[:, None, None, None]
    gi = jnp.arange(G)[None, :, None, None]

    def block(i):
        q0 = i * Q_BLOCK
        qb = lax.dynamic_slice_in_dim(q, q0, Q_BLOCK, axis=3)
        gb = lax.dynamic_slice_in_dim(gates, q0, Q_BLOCK, axis=3)
        qpos = q0 + jnp.arange(Q_BLOCK)
        sc = jnp.einsum('bghqd,bgnd->bghqn', qb, k_cmp).astype(jnp.float32) * scale
        cmask = cmp_end[None, :] <= qpos[:, None]
        has_cmp = jnp.any(cmask, axis=-1)[:, None].astype(jnp.float32)
        p_cmp = jax.nn.softmax(jnp.where(cmask, sc, NEG_INF), axis=-1) * has_cmp
        o_cmp = jnp.einsum('bghqn,bgnd->bghqd', p_cmp.astype(v_cmp.dtype), v_cmp)
        imp = jnp.einsum('bghqn,ns->bgqs', p_cmp, ov)
        cur = (qpos // SLC_LEN)[:, None]
        forced = (blk[None, :] == 0) | (blk[None, :] == cur) | (blk[None, :] == cur - 1)
        imp = jnp.where(forced, BIG, imp)
        imp = jnp.where(blk[None, :] > cur, NEG_INF, imp)
        _, idx = lax.top_k(imp, topk)
        kb = k_slc[bi, gi, idx]
        vb = v_slc[bi, gi, idx]
        tok = idx[..., None] * SLC_LEN + jnp.arange(SLC_LEN)
        smask = (tok <= qpos[:, None, None])[:, :, None]
        ss = jnp.einsum('bghqd,bgqnld->bghqnl', qb, kb).astype(jnp.float32) * scale
        ss = jnp.where(smask, ss, NEG_INF)
        ps = jax.nn.softmax(ss.reshape(ss.shape[:4] + (-1,)), axis=-1).reshape(ss.shape)
        o_slc = jnp.einsum('bghqnl,bgqnld->bghqd', ps.astype(vb.dtype), vb)
        kw = lax.dynamic_slice_in_dim(k_win_pad, q0, Q_BLOCK + WIN, axis=2)
        vw = lax.dynamic_slice_in_dim(v_win_pad, q0, Q_BLOCK + WIN, axis=2)
        kpos = q0 - WIN + jnp.arange(Q_BLOCK + WIN)
        wmask = ((kpos[None, :] <= qpos[:, None]) & (kpos[None, :] > qpos[:, None] - WIN)
                 & (kpos[None, :] >= 0))
        sw = jnp.einsum('bghqd,bgkd->bghqk', qb, kw).astype(jnp.float32) * scale
        pw = jax.nn.softmax(jnp.where(wmask, sw, NEG_INF), axis=-1)
        o_win = jnp.einsum('bghqk,bgkd->bghqd', pw.astype(vw.dtype), vw)
        return gb[..., 0:1] * o_cmp + gb[..., 1:2] * o_slc + gb[..., 2:3] * o_win

    out = lax.map(block, jnp.arange(S // Q_BLOCK))
    return out.transpose(1, 2, 3, 0, 4, 5).reshape(Bn, G, Hg, S, -1)


def pool_mixer(u, w_grp, scale):
    Bn, S, C = u.shape
    ug = u.reshape(Bn, S, C_GROUPS, C_GDIM).astype(jnp.float32)
    cs = jnp.pad(jnp.cumsum(ug, axis=1), ((0, 0), (1, 0), (0, 0), (0, 0)))
    t = jnp.arange(S)
    means = []
    for g, w in enumerate(C_WINDOWS):
        start = jnp.maximum(t + 1 - w, 0)
        cnt = (t + 1 - start).astype(jnp.float32)
        means.append((cs[:, 1:, g] - cs[:, start, g]) / cnt[None, :, None])
    pooled = (jnp.stack(means, axis=2) - ug).astype(u.dtype)
    y = jnp.einsum('bsgc,gcd->bsgd', pooled, w_grp)
    return y.reshape(Bn, S, C) * scale


def setup_inputs(seed: int = 0) -> dict:
    key = jax.random.key(seed)
    ks = jax.random.split(key, 20)
    L, D = DEPTH, D_MODEL
    nrm = lambda k, shape, s: jax.random.normal(k, shape, jnp.float32) * s
    return {
        'x': nrm(ks[0], (BATCH, SEQ, D), 1.0),
        'norm_g': 1.0 + nrm(ks[1], (L, D), 0.1),
        'w_in': nrm(ks[2], (L, D, N_IN), D ** -0.5),
        'a_q_g': 1.0 + nrm(ks[3], (L, A_DK), 0.1),
        'a_k_g': 1.0 + nrm(ks[4], (L, A_DK), 0.1),
        'a_lam': nrm(ks[5], (L, 4, A_DK), 0.1),
        'a_subln_g': 1.0 + nrm(ks[6], (L, A_DV), 0.1),
        'b_q_g': 1.0 + nrm(ks[7], (L, B_DK), 0.1),
        'b_k_g': 1.0 + nrm(ks[8], (L, N_NSA_BRANCH, B_DK), 0.1),
        'b_cmp_pos': nrm(ks[9], (L, 2, CMP_LEN, B_DK), 0.1),
        'b_phi_w1': nrm(ks[10], (L, 2, CMP_LEN * B_DK, PHI_HIDDEN), (CMP_LEN * B_DK) ** -0.5),
        'b_phi_w2': nrm(ks[11], (L, 2, PHI_HIDDEN, B_DK), PHI_HIDDEN ** -0.5),
        'c_w': nrm(ks[12], (L, C_GROUPS, C_GDIM, C_GDIM), C_GDIM ** -0.5),
        'c_scale': 1.0 + nrm(ks[13], (L, C_WIDTH), 0.1),
        'w_branch': nrm(ks[14], (L, N_BRANCH, BRANCH_WIDTH, D), BRANCH_WIDTH ** -0.5),
        'w_out': nrm(ks[15], (L, D, D), D ** -0.5),
    }


def reference(x, norm_g, w_in, a_q_g, a_k_g, a_lam, a_subln_g, b_q_g, b_k_g, b_cmp_pos,
              b_phi_w1, b_phi_w2, c_w, c_scale, w_branch, w_out):
    Bn, S, D = x.shape
    split_at = [int(v) for v in np.cumsum(SPLIT_SIZES)[:-1]]
    for l in range(DEPTH):
        h = rms_norm(x, norm_g[l])
        z = h @ w_in[l]
        aq, ak, av, ag, bq, bkv, bbg, bg, cu, cg, mg = jnp.split(z, split_at, axis=-1)

        qa = rms_norm(aq.reshape(Bn, S, A_HEADS, 2, A_DK), a_q_g[l]).transpose(0, 2, 3, 1, 4)
        ka = rms_norm(ak.reshape(Bn, S, A_HEADS, 2, A_DK), a_k_g[l]).transpose(0, 2, 3, 1, 4)
        va = av.reshape(Bn, S, A_HEADS, A_DV).transpose(0, 2, 1, 3)
        lam_init = 0.8 - 0.6 * math.exp(-0.3 * l)
        lp = a_lam[l].astype(jnp.float32)
        lam = jnp.exp(jnp.sum(lp[0] * lp[1])) - jnp.exp(jnp.sum(lp[2] * lp[3])) + lam_init
        oa = diff_attention(qa, ka, va, lam).transpose(0, 2, 1, 3)
        oa = (rms_norm(oa, a_subln_g[l]) * (1.0 - lam_init)).reshape(Bn, S, BRANCH_WIDTH)
        ya = (oa * jax.nn.silu(ag)) @ w_branch[l, 0]

        qb = rms_norm(bq.reshape(Bn, S, B_GROUPS, B_HPG, B_DK), b_q_g[l]).transpose(0, 2, 3, 1, 4)
        kv = bkv.reshape(Bn, S, N_NSA_BRANCH, 2, B_GROUPS, B_DK).transpose(2, 3, 0, 4, 1, 5)
        gates = jax.nn.sigmoid(bbg.reshape(Bn, S, B_GROUPS, B_HPG, N_NSA_BRANCH)).transpose(0, 2, 3, 1, 4)
        k_cmp = rms_norm(compress_blocks(kv[0, 0], b_cmp_pos[l, 0], b_phi_w1[l, 0], b_phi_w2[l, 0]), b_k_g[l, 0])
        v_cmp = compress_blocks(kv[0, 1], b_cmp_pos[l, 1], b_phi_w1[l, 1], b_phi_w2[l, 1])
        ns = S // SLC_LEN
        k_slc = rms_norm(kv[1, 0], b_k_g[l, 1]).reshape(Bn, B_GROUPS, ns, SLC_LEN, B_DK)
        v_slc = kv[1, 1].reshape(Bn, B_GROUPS, ns, SLC_LEN, B_DV)
        pad = ((0, 0), (0, 0), (WIN, 0), (0, 0))
        k_win = jnp.pad(rms_norm(kv[2, 0], b_k_g[l, 2]), pad)
        v_win = jnp.pad(kv[2, 1], pad)
        ob = nsa_attention(qb, k_cmp, v_cmp, k_slc, v_slc, k_win, v_win, gates)
        ob = ob.transpose(0, 3, 1, 2, 4).reshape(Bn, S, BRANCH_WIDTH)
        yb = (ob * jax.nn.silu(bg)) @ w_branch[l, 1]

        oc = pool_mixer(cu, c_w[l], c_scale[l])
        yc = (oc * jax.nn.silu(cg)) @ w_branch[l, 2]

        g = jax.nn.sigmoid(mg.reshape(Bn, S, N_BRANCH, D))
        merged = g[:, :, 0] * ya + g[:, :, 1] * yb + g[:, :, 2] * yc
        x = x + merged @ w_out[l]
    return x
```

```python
import functools
import math

import numpy as np
import jax
import jax.numpy as jnp
from jax import lax
from jax.experimental import pallas as pl
from jax.experimental.pallas import tpu as pltpu

F32 = jnp.float32
BF16 = jnp.bfloat16

D_MODEL = 1024
DEPTH = 2
BRANCH_WIDTH = D_MODEL // 2
A_DK = 64
A_DV = 2 * A_DK
A_HEADS = BRANCH_WIDTH // A_DV
B_DK = 64
B_HEADS = BRANCH_WIDTH // B_DK
B_GROUPS = 2
B_HPG = B_HEADS // B_GROUPS
CMP_LEN = 32
CMP_STRIDE = 16
SLC_LEN = 64
SLC_TOPK = 16
WIN = 512
PHI_HIDDEN = 256
N_NSA_BRANCH = 3
C_WINDOWS = (2, 4, 8, 16)
C_GDIM = BRANCH_WIDTH // len(C_WINDOWS)
N_BRANCH = 3
EPS = 1e-6
NEG_INF = -1e30
BIG = 1e30

LANES = 128
HALO = 16
VMEM_LIMIT = 48 * 1024 * 1024

ZW = 8192
BLK_AQ, BLK_AK, BLK_AV, BLK_AG = 0, 4, 8, 12
BLK_BQ, BLK_BG, BLK_CU, BLK_CG = 16, 20, 24, 28
BLK_MG = 32
BLK_BKV = 56
BLK_BBG = 62

SCALE = 0.125


def _permute_w_in(w):
    o = np.cumsum([0, 512, 512, 512, 512, 512, 768, 24, 512, 512, 512, 3072])
    seg = lambda i: w[:, int(o[i]):int(o[i + 1])]
    aq, ak, av, ag, bq, bkv, bbg, bg, cu, cg, mg = [seg(i) for i in range(11)]
    parts = [aq, ak, av, ag, bq, bg, cu, cg, mg, bkv, bbg]
    used = sum(p.shape[1] for p in parts)
    parts.append(jnp.zeros((w.shape[0], ZW - used), w.dtype))
    return jnp.concatenate(parts, axis=1)


def _half_norm(x, gain, lo):
    x2 = x * x
    s_lo = jnp.sum(jnp.where(lo, x2, 0.0), axis=-1, keepdims=True)
    s_hi = jnp.sum(jnp.where(lo, 0.0, x2), axis=-1, keepdims=True)
    ms = jnp.where(lo, s_lo, s_hi) * (1.0 / 64.0)
    return x * lax.rsqrt(ms + EPS) * gain


def _inproj_kernel(x_ref, g_ref, w_ref, o_ref, h_ref):
    @pl.when(pl.program_id(1) == 0)
    def _():
        x = x_ref[...]
        ms = jnp.mean(x * x, axis=-1, keepdims=True)
        h_ref[...] = (x * lax.rsqrt(ms + EPS) * g_ref[...]).astype(BF16)

    o_ref[...] = jnp.dot(h_ref[...], w_ref[...], preferred_element_type=F32).astype(o_ref.dtype)


def _inproj(x2d, g, w):
    rows, d = x2d.shape
    tm, tn = min(1024, rows), 1024
    return pl.pallas_call(
        _inproj_kernel,
        grid=(rows // tm, ZW // tn),
        in_specs=[pl.BlockSpec((tm, d), lambda i, j: (i, 0)),
                  pl.BlockSpec((1, d), lambda i, j: (0, 0)),
                  pl.BlockSpec((d, tn), lambda i, j: (0, j))],
        out_specs=pl.BlockSpec((tm, tn), lambda i, j: (i, j)),
        out_shape=jax.ShapeDtypeStruct((rows, ZW), BF16),
        scratch_shapes=[pltpu.VMEM((tm, d), BF16)],
        compiler_params=pltpu.CompilerParams(
            dimension_semantics=("parallel", "arbitrary"), vmem_limit_bytes=VMEM_LIMIT),
        name="inproj",
    )(x2d, g, w)


def _softmax_step(s, v_tile, m_ref, l_ref, acc_ref):
    m_old = m_ref[...]
    m_new = jnp.maximum(m_old, jnp.max(s, axis=-1, keepdims=True))
    alpha = jnp.exp(m_old - m_new)
    p = jnp.exp(s - m_new)
    l_ref[...] = alpha * l_ref[...] + jnp.sum(p, axis=-1, keepdims=True)
    acc_ref[...] = alpha * acc_ref[...] + jnp.dot(p.astype(BF16), v_tile,
                                                  preferred_element_type=F32)
    m_ref[...] = m_new


def _reset(m_ref, l_ref, acc_ref):
    m_ref[...] = jnp.full(m_ref.shape, NEG_INF, F32)
    l_ref[...] = jnp.zeros(l_ref.shape, F32)
    acc_ref[...] = jnp.zeros(acc_ref.shape, F32)


def _qk(q, k_tile):
    return lax.dot_general(q, k_tile, (((1,), (1,)), ((), ())), preferred_element_type=F32)


def _normalize_keys(src_ref, dst_ref, gain, lo, seq, chunk=512):
    def body(c, carry):
        r = pl.ds(pl.multiple_of(c * chunk, chunk), chunk)
        dst_ref[r, :] = _half_norm(src_ref[r, :].astype(F32), gain, lo).astype(BF16)
        return carry
    lax.fori_loop(0, seq // chunk, body, 0)


def _diff_attn_kernel(lam_ref, q_ref, k_ref, v_ref, qg_ref, kg_ref, sg_ref, o_ref,
                      kn_ref, m_ref, l_ref, acc_ref, *, tq, tk, seq, lam_init):
    qi = pl.program_id(2)
    lo = lax.broadcasted_iota(jnp.int32, (1, LANES), 1) < 64

    @pl.when(qi == 0)
    def _():
        _normalize_keys(k_ref, kn_ref, kg_ref[...], lo, seq)

    qn = _half_norm(q_ref[...].astype(F32), qg_ref[...], lo) * SCALE
    qq = jnp.concatenate([jnp.where(lo, qn, 0.0), jnp.where(lo, 0.0, qn)], axis=0).astype(BF16)

    _reset(m_ref, l_ref, acc_ref)
    q0 = qi * tq
    nfull = q0 // tk

    def tile(j, masked):
        r = pl.ds(pl.multiple_of(j * tk, tk), tk)
        s = _qk(qq, kn_ref[r, :])
        if masked:
            row = lax.broadcasted_iota(jnp.int32, (2 * tq, tk), 0)
            qpos = q0 + jnp.where(row >= tq, row - tq, row)
            kpos = j * tk + lax.broadcasted_iota(jnp.int32, (2 * tq, tk), 1)
            s = jnp.where(kpos <= qpos, s, NEG_INF)
        _softmax_step(s, v_ref[r, :], m_ref, l_ref, acc_ref)

    def body(j, carry):
        tile(j, False)
        return carry
    lax.fori_loop(0, nfull, body, 0)
    tile(nfull, True)

    lp = lam_ref[...]
    lam = (jnp.exp(jnp.sum(lp[0:1] * lp[1:2], axis=-1, keepdims=True))
           - jnp.exp(jnp.sum(lp[2:3] * lp[3:4], axis=-1, keepdims=True)) + lam_init)
    acc = acc_ref[...]
    l = l_ref[...]
    o = acc[:tq] / l[:tq] - lam * (acc[tq:] / l[tq:])
    ms = jnp.mean(o * o, axis=-1, keepdims=True)
    o_ref[...] = o * lax.rsqrt(ms + EPS) * sg_ref[...] * (1.0 - lam_init)


def _diff_attn(z, lam_p, qg, kg, sg, *, batch, seq, lam_init):
    tq = min(256, seq)
    tk = min(512, seq)
    nq = seq // tq
    kern = functools.partial(_diff_attn_kernel, tq=tq, tk=tk, seq=seq, lam_init=lam_init)
    vec = lambda: pl.BlockSpec((1, LANES), lambda b, h, i: (0, 0))
    return pl.pallas_call(
        kern,
        grid=(batch, A_HEADS, nq),
        in_specs=[pl.BlockSpec((4, A_DK), lambda b, h, i: (0, 0)),
                  pl.BlockSpec((tq, LANES), lambda b, h, i: (b * nq + i, BLK_AQ + h)),
                  pl.BlockSpec((seq, LANES), lambda b, h, i: (b, BLK_AK + h)),
                  pl.BlockSpec((seq, LANES), lambda b, h, i: (b, BLK_AV + h)),
                  vec(), vec(), vec()],
        out_specs=pl.BlockSpec((tq, LANES), lambda b, h, i: (b * nq + i, h)),
        out_shape=jax.ShapeDtypeStruct((batch * seq, BRANCH_WIDTH), F32),
        scratch_shapes=[pltpu.VMEM((seq, LANES), BF16),
                        pltpu.VMEM((2 * tq, 1), F32),
                        pltpu.VMEM((2 * tq, 1), F32),
                        pltpu.VMEM((2 * tq, LANES), F32)],
        compiler_params=pltpu.CompilerParams(
            dimension_semantics=("parallel", "parallel", "arbitrary"),
            vmem_limit_bytes=VMEM_LIMIT),
        name="diff_attn",
    )(lam_p, z, z, z, qg, kg, sg)


def _compress_kernel(blk_ref, pos_ref, w1_ref, w2_ref, g_ref, o_ref):
    flat = (blk_ref[...].astype(F32) + pos_ref[...]).astype(BF16)
    h = jnp.dot(flat, w1_ref[...], preferred_element_type=F32)
    h = h * jax.nn.sigmoid(h)
    y = jnp.dot(h.astype(BF16), w2_ref[...], preferred_element_type=F32)

    @pl.when(pl.program_id(0) == 0)
    def _():
        ms = jnp.mean(y * y, axis=-1, keepdims=True)
        o_ref[...] = (y * lax.rsqrt(ms + EPS) * g_ref[...]).astype(o_ref.dtype)

    @pl.when(pl.program_id(0) != 0)
    def _():
        o_ref[...] = y.astype(o_ref.dtype)


def _compress(blocks, pos, w1, w2, gain):
    _, batch, groups, ncp, flat = blocks.shape
    sq = lambda *s: pl.BlockSpec((None,) * (5 - len(s)) + s, lambda t, b, g: (t, b, g, 0, 0))
    return pl.pallas_call(
        _compress_kernel,
        grid=(2, batch, groups),
        in_specs=[sq(ncp, flat),
                  pl.BlockSpec((None, 1, flat), lambda t, b, g: (t, 0, 0)),
                  pl.BlockSpec((None, flat, PHI_HIDDEN), lambda t, b, g: (t, 0, 0)),
                  pl.BlockSpec((None, PHI_HIDDEN, B_DK), lambda t, b, g: (t, 0, 0)),
                  pl.BlockSpec((1, B_DK), lambda t, b, g: (0, 0))],
        out_specs=sq(ncp, B_DK),
        out_shape=jax.ShapeDtypeStruct((2, batch, groups, ncp, B_DK), BF16),
        compiler_params=pltpu.CompilerParams(
            dimension_semantics=("arbitrary", "arbitrary", "arbitrary"),
            vmem_limit_bytes=VMEM_LIMIT),
        name="compress",
    )(blocks, pos, w1, w2, gain)


def _split3_bf16(x):
    hi = x.astype(BF16)
    r1 = x - hi.astype(F32)
    mid = r1.astype(BF16)
    lo = (r1 - mid.astype(F32)).astype(BF16)
    return hi, mid, lo


def _nsa_kernel(q_ref, kc_ref, vc_ref, ks_ref, vs_ref, kw_ref, vw_ref, gt_ref, ovt_ref,
                qg_ref, ksg_ref, kwg_ref, o_ref,
                ksn_ref, kwn_ref, m_ref, l_ref, acc_ref, *, tq, tk, seq):
    g = pl.program_id(1)
    qi = pl.program_id(2)
    ns = seq // SLC_LEN
    ncp = seq // CMP_STRIDE
    rows = B_HPG * tq
    lane = lax.broadcasted_iota(jnp.int32, (1, LANES), 1)
    lo = lane < 64
    gv = jnp.zeros((1, LANES), jnp.int32) + g
    mine = (lane >> 6) == gv
    q0 = qi * tq

    @pl.when(qi == 0)
    def _():
        _normalize_keys(ks_ref, ksn_ref, ksg_ref[...], lo, seq)
        _normalize_keys(kw_ref, kwn_ref, kwg_ref[...], lo, seq)

    heads = []
    for pair in range(B_HPG // 2):
        x = _half_norm(q_ref[:, pair * LANES:(pair + 1) * LANES].astype(F32), qg_ref[...], lo) * SCALE
        xr = pltpu.roll(x, 64, 1)
        for half in range(2):
            heads.append(jnp.where(mine, jnp.where(gv == half, x, xr), 0.0))
    qs = jnp.concatenate(heads, axis=0).astype(BF16)

    row = lax.broadcasted_iota(jnp.int32, (rows, 1), 0)
    qpos_col = q0 + (row & (tq - 1))

    s = _qk(qs, kc_ref[...])
    cmp_end = lax.broadcasted_iota(jnp.int32, (1, ncp), 1) * CMP_STRIDE + (CMP_LEN - 1)
    cmask = cmp_end <= qpos_col
    s = jnp.where(cmask, s, NEG_INF)
    e = jnp.where(cmask, jnp.exp(s - jnp.max(s, axis=-1, keepdims=True)), 0.0)
    den = jnp.sum(e, axis=-1, keepdims=True)
    p_cmp = e / jnp.where(den > 0.0, den, 1.0)
    o_cmp = jnp.dot(p_cmp.astype(BF16), vc_ref[...], preferred_element_type=F32)

    psum = p_cmp[0:tq]
    for h in range(1, B_HPG):
        psum = psum + p_cmp[h * tq:(h + 1) * tq]
    ovt = ovt_ref[...]
    imp_t = sum(lax.dot_general(ovt, part, (((1,), (1,)), ((), ())), preferred_element_type=F32)
                for part in _split3_bf16(psum))
    blk = lax.broadcasted_iota(jnp.int32, (ns, tq), 0)
    cur = (q0 + lax.broadcasted_iota(jnp.int32, (ns, tq), 1)) // SLC_LEN
    val = jnp.where((blk == 0) | (blk == cur) | (blk == cur - 1), BIG, imp_t)
    val = jnp.where(blk > cur, NEG_INF, val)
    groups8 = [val[8 * r:8 * r + 8] for r in range(ns // 8)]
    sub = lax.broadcasted_iota(jnp.int32, (8, tq), 0)
    ranks = [jnp.zeros((8, tq), F32) for _ in range(ns // 8)]
    for jp in range(ns):
        other = jnp.broadcast_to(val[jp:jp + 1], (8, tq))
        for r in range(ns // 8):
            ge = jnp.where(other >= groups8[r], 1.0, 0.0)
            gt = jnp.where(other > groups8[r], 1.0, 0.0)
            if 8 * r > jp:
                inc = ge
            elif 8 * r + 7 <= jp:
                inc = gt
            else:
                inc = jnp.where(sub + 8 * r > jp, ge, gt)
            ranks[r] = ranks[r] + inc
    rank = jnp.concatenate(ranks, axis=0)
    notsel = jnp.where(rank < float(min(SLC_TOPK, ns)), 0.0, 1.0).T.astype(BF16)

    _reset(m_ref, l_ref, acc_ref)
    nfull = q0 // tk

    def slc_tile(j, masked):
        r = pl.ds(pl.multiple_of(j * tk, tk), tk)
        kblk = (j * tk + lax.broadcasted_iota(jnp.int32, (ns, tk), 1)) // SLC_LEN
        eb = jnp.where(lax.broadcasted_iota(jnp.int32, (ns, tk), 0) == kblk, NEG_INF, 0.0).astype(BF16)
        bias = jnp.dot(notsel, eb, preferred_element_type=F32)
        if masked:
            kpos = j * tk + lax.broadcasted_iota(jnp.int32, (tq, tk), 1)
            qpos = q0 + lax.broadcasted_iota(jnp.int32, (tq, tk), 0)
            bias = jnp.where(kpos <= qpos, bias, NEG_INF)
        sc = _qk(qs, ksn_ref[r, :]).reshape(B_HPG, tq, tk) + bias[None]
        _softmax_step(sc.reshape(rows, tk), vs_ref[r, :], m_ref, l_ref, acc_ref)

    def slc_body(j, carry):
        slc_tile(j, False)
        return carry
    lax.fori_loop(0, nfull, slc_body, 0)
    slc_tile(nfull, True)
    o_slc = acc_ref[...] / l_ref[...]

    _reset(m_ref, l_ref, acc_ref)
    tw = tq
    nwin = WIN // tw

    def win_tile(t, kind):
        r = pl.ds(pl.multiple_of(t * tw, tw), tw)
        sc = _qk(qs, kwn_ref[r, :])
        if kind != "full":
            kl = lax.broadcasted_iota(jnp.int32, (rows, tw), 1)
            ql = lax.broadcasted_iota(jnp.int32, (rows, tw), 0) & (tq - 1)
            keep = (kl <= ql) if kind == "diag" else (kl > ql)
            sc = jnp.where(keep, sc, NEG_INF)
        _softmax_step(sc, vw_ref[r, :], m_ref, l_ref, acc_ref)

    win_tile(qi, "diag")
    for d in range(1, nwin):
        @pl.when(qi >= d)
        def _():
            win_tile(qi - d, "full")

    @pl.when(qi >= nwin)
    def _():
        win_tile(qi - nwin, "band")
    o_win = acc_ref[...] / l_ref[...]

    gates = jax.nn.sigmoid(gt_ref[...].astype(F32))

    def gate_col(br):
        cols = []
        for h in range(B_HPG):
            c = g * (B_HPG * N_NSA_BRANCH) + h * N_NSA_BRANCH + br
            cols.append(jnp.sum(jnp.where(lane == c, gates, 0.0), axis=-1, keepdims=True))
        return jnp.concatenate(cols, axis=0)

    o = gate_col(0) * o_cmp + gate_col(1) * o_slc + gate_col(2) * o_win
    for pair in range(B_HPG // 2):
        placed = []
        for half in range(2):
            oh = o[(2 * pair + half) * tq:(2 * pair + half + 1) * tq]
            placed.append(jnp.where(gv == half, oh, pltpu.roll(oh, 64, 1)))
        o_ref[:, pair * LANES:(pair + 1) * LANES] = jnp.where(lo, placed[0], placed[1])


def _nsa(z, kc, vc, ovt, qg, ksg, kwg, *, batch, seq):
    tq = min(256, seq)
    tk = min(512, seq)
    nq = seq // tq
    ns = seq // SLC_LEN
    ncp = seq // CMP_STRIDE
    rows = B_HPG * tq
    kern = functools.partial(_nsa_kernel, tq=tq, tk=tk, seq=seq)
    kv = lambda blk: pl.BlockSpec((seq, LANES), lambda b, g, i: (b, blk))
    vec = lambda: pl.BlockSpec((1, LANES), lambda b, g, i: (0, 0))
    cmp_spec = lambda: pl.BlockSpec((None, ncp, LANES), lambda b, g, i: (b, 0, 0))
    return pl.pallas_call(
        kern,
        grid=(batch, B_GROUPS, nq),
        in_specs=[pl.BlockSpec((tq, 2 * LANES), lambda b, g, i: (b * nq + i, BLK_BQ // 2 + g)),
                  cmp_spec(), cmp_spec(),
                  kv(BLK_BKV + 2), kv(BLK_BKV + 3), kv(BLK_BKV + 4), kv(BLK_BKV + 5),
                  pl.BlockSpec((tq, LANES), lambda b, g, i: (b * nq + i, BLK_BBG)),
                  pl.BlockSpec((ns, ncp), lambda b, g, i: (0, 0)),
                  vec(), vec(), vec()],
        out_specs=pl.BlockSpec((tq, 2 * LANES), lambda b, g, i: (b * nq + i, g)),
        out_shape=jax.ShapeDtypeStruct((batch * seq, BRANCH_WIDTH), F32),
        scratch_shapes=[pltpu.VMEM((seq, LANES), BF16),
                        pltpu.VMEM((seq, LANES), BF16),
                        pltpu.VMEM((rows, 1), F32),
                        pltpu.VMEM((rows, 1), F32),
                        pltpu.VMEM((rows, LANES), F32)],
        compiler_params=pltpu.CompilerParams(
            dimension_semantics=("parallel", "parallel", "arbitrary"),
            vmem_limit_bytes=VMEM_LIMIT),
        name="nsa",
    )(z, kc, vc, z, z, z, z, z, ovt, qg, ksg, kwg)


def _merge_kernel(x_ref, oa_ref, ob_ref, ag_ref, bg_ref, cu_ref, halo_ref, cg_ref,
                  mg0_ref, mg1_ref, mg2_ref, cw_ref, cs_ref, wb_ref, wo_ref, o_ref,
                  ext_ref, *, tm, seq):
    i = pl.program_id(0)
    tpos0 = (i * tm) % seq

    ext_ref[HALO:, :] = cu_ref[...].astype(F32)
    ext_ref[:HALO, :] = jnp.where(tpos0 == 0, 0.0, halo_ref[...].astype(F32))
    tpos = tpos0 + lax.broadcasted_iota(jnp.int32, (tm, 1), 0)
    pooled = []
    for gi, w in enumerate(C_WINDOWS):
        c = slice(gi * C_GDIM, (gi + 1) * C_GDIM)
        cur = ext_ref[HALO:, c]
        tot = cur
        for d in range(1, w):
            tot = tot + ext_ref[HALO - d:HALO - d + tm, c]
        cnt = jnp.minimum(tpos + 1, w).astype(F32)
        pg = (tot / cnt - cur).astype(BF16)
        pooled.append(jnp.dot(pg, cw_ref[gi], preferred_element_type=F32))
    oc = jnp.concatenate(pooled, axis=1) * cs_ref[...]

    def gated(o, gate_ref, k):
        gt = gate_ref[...].astype(F32)
        return jnp.dot((o * (gt * jax.nn.sigmoid(gt))).astype(BF16), wb_ref[k],
                       preferred_element_type=F32)

    merged = (jax.nn.sigmoid(mg0_ref[...].astype(F32)) * gated(oa_ref[...], ag_ref, 0)
              + jax.nn.sigmoid(mg1_ref[...].astype(F32)) * gated(ob_ref[...], bg_ref, 1)
              + jax.nn.sigmoid(mg2_ref[...].astype(F32)) * gated(oc, cg_ref, 2))
    o_ref[...] = x_ref[...] + jnp.dot(merged.astype(BF16), wo_ref[...], preferred_element_type=F32)


def _merge(x2d, z, oa, ob, cw, cs, wb, wo, *, seq):
    rows, d = x2d.shape
    tm = min(512, seq)
    bw = BRANCH_WIDTH
    kern = functools.partial(_merge_kernel, tm=tm, seq=seq)
    zcol = lambda blk: pl.BlockSpec((tm, bw), lambda i: (i, blk * LANES // bw))
    mgs = lambda k: pl.BlockSpec((tm, d), lambda i: (i, BLK_MG * LANES // d + k))
    full = lambda a: pl.BlockSpec(a.shape, lambda i: (0,) * a.ndim)
    return pl.pallas_call(
        kern,
        grid=(rows // tm,),
        in_specs=[pl.BlockSpec((tm, d), lambda i: (i, 0)),
                  pl.BlockSpec((tm, bw), lambda i: (i, 0)),
                  pl.BlockSpec((tm, bw), lambda i: (i, 0)),
                  zcol(BLK_AG), zcol(BLK_BG), zcol(BLK_CU),
                  pl.BlockSpec((HALO, bw),
                               lambda i: (jnp.maximum(i * (tm // HALO) - 1, 0), BLK_CU * LANES // bw)),
                  zcol(BLK_CG), mgs(0), mgs(1), mgs(2),
                  full(cw), full(cs), full(wb), full(wo)],
        out_specs=pl.BlockSpec((tm, d), lambda i: (i, 0)),
        out_shape=jax.ShapeDtypeStruct((rows, d), F32),
        scratch_shapes=[pltpu.VMEM((tm + HALO, bw), F32)],
        compiler_params=pltpu.CompilerParams(
            dimension_semantics=("parallel",), vmem_limit_bytes=VMEM_LIMIT),
        name="merge",
    )(x2d, oa, ob, z, z, z, z, z, z, z, z, cw, cs, wb, wo)


def _overlap_t(seq):
    ncp = seq // CMP_STRIDE
    nc = (seq - CMP_LEN) // CMP_STRIDE + 1
    ns = seq // SLC_LEN
    cs = np.arange(ncp) * CMP_STRIDE
    ce = cs + CMP_LEN
    ss = np.arange(ns) * SLC_LEN
    se = ss + SLC_LEN
    ov = np.clip(np.minimum(ce[None, :], se[:, None]) - np.maximum(cs[None, :], ss[:, None]), 0, None)
    ov = ov / CMP_LEN
    ov[:, nc:] = 0.0
    return jnp.asarray(ov, dtype=BF16)


def _cmp_blocks(z3, blk, seq):
    batch = z3.shape[0]
    ncp = seq // CMP_STRIDE
    a = z3[:, :, blk * LANES:(blk + 1) * LANES]
    a = a.reshape(batch, ncp, CMP_STRIDE, B_GROUPS, B_DK).transpose(0, 3, 1, 2, 4)
    a = a.reshape(batch, B_GROUPS, ncp, CMP_STRIDE * B_DK)
    nxt = jnp.concatenate([a[:, :, 1:], jnp.zeros_like(a[:, :, :1])], axis=2)
    return jnp.concatenate([a, nxt], axis=-1)


def kernel(x, norm_g, w_in, a_q_g, a_k_g, a_lam, a_subln_g, b_q_g, b_k_g, b_cmp_pos, b_phi_w1,
           b_phi_w2, c_w, c_scale, w_branch, w_out):
    batch, seq, d = x.shape
    x2d = x.reshape(batch * seq, d)
    ovt = _overlap_t(seq)
    two = lambda v: jnp.concatenate([v, v]).reshape(1, LANES).astype(F32)
    for l in range(DEPTH):
        z = _inproj(x2d, norm_g[l].reshape(1, d), _permute_w_in(w_in[l]).astype(BF16))
        lam_init = 0.8 - 0.6 * math.exp(-0.3 * l)
        oa = _diff_attn(z, a_lam[l].astype(F32), two(a_q_g[l]), two(a_k_g[l]),
                        a_subln_g[l].reshape(1, LANES).astype(F32),
                        batch=batch, seq=seq, lam_init=lam_init)
        z3 = z.reshape(batch, seq, ZW)
        blocks = jnp.stack([_cmp_blocks(z3, BLK_BKV, seq), _cmp_blocks(z3, BLK_BKV + 1, seq)])
        cmp = _compress(blocks, b_cmp_pos[l].reshape(2, 1, CMP_LEN * B_DK).astype(F32),
                        b_phi_w1[l].astype(BF16), b_phi_w2[l].astype(BF16),
                        b_k_g[l, 0].reshape(1, B_DK).astype(F32))
        kc = jnp.concatenate([cmp[0, :, 0], cmp[0, :, 1]], axis=-1)
        vc = jnp.concatenate([cmp[1, :, 0], cmp[1, :, 1]], axis=-1)
        ob = _nsa(z, kc, vc, ovt, two(b_q_g[l]), two(b_k_g[l, 1]), two(b_k_g[l, 2]),
                  batch=batch, seq=seq)
        x2d = _merge(x2d, z, oa, ob, c_w[l].astype(BF16), c_scale[l].reshape(1, BRANCH_WIDTH),
                     w_branch[l].astype(BF16), w_out[l].astype(BF16), seq=seq)
    return x2d.reshape(batch, seq, d)
```

```python
import functools
import math

import numpy as np
import jax
import jax.numpy as jnp
from jax import lax
from jax.experimental import pallas as pl
from jax.experimental.pallas import tpu as pltpu

F32 = jnp.float32
BF16 = jnp.bfloat16

D_MODEL = 1024
DEPTH = 2
BRANCH_WIDTH = D_MODEL // 2
A_DK = 64
A_DV = 2 * A_DK
A_HEADS = BRANCH_WIDTH // A_DV
B_DK = 64
B_HEADS = BRANCH_WIDTH // B_DK
B_GROUPS = 2
B_HPG = B_HEADS // B_GROUPS
CMP_LEN = 32
CMP_STRIDE = 16
SLC_LEN = 64
SLC_TOPK = 16
WIN = 512
PHI_HIDDEN = 256
N_NSA_BRANCH = 3
C_WINDOWS = (2, 4, 8, 16)
C_GDIM = BRANCH_WIDTH // len(C_WINDOWS)
N_BRANCH = 3
EPS = 1e-6
NEG_INF = -1e30
BIG = 1e30

LANES = 128
HALO = 16
VMEM_LIMIT = 48 * 1024 * 1024

ZW = 8192
BLK_AQ, BLK_AK, BLK_AV, BLK_AG = 0, 4, 8, 12
BLK_BQ, BLK_BG, BLK_CU, BLK_CG = 16, 20, 24, 28
BLK_MG = 32
BLK_BKV = 56
BLK_BBG = 62

SCALE = 0.125


def _permute_w_in(w):
    o = np.cumsum([0, 512, 512, 512, 512, 512, 768, 24, 512, 512, 512, 3072])
    seg = lambda i: w[:, int(o[i]):int(o[i + 1])]
    aq, ak, av, ag, bq, bkv, bbg, bg, cu, cg, mg = [seg(i) for i in range(11)]
    parts = [aq, ak, av, ag, bq, bg, cu, cg, mg, bkv, bbg]
    used = sum(p.shape[1] for p in parts)
    parts.append(jnp.zeros((w.shape[0], ZW - used), w.dtype))
    return jnp.concatenate(parts, axis=1)


def _half_norm(x, gain, lo):
    x2 = x * x
    s_lo = jnp.sum(jnp.where(lo, x2, 0.0), axis=-1, keepdims=True)
    s_hi = jnp.sum(jnp.where(lo, 0.0, x2), axis=-1, keepdims=True)
    ms = jnp.where(lo, s_lo, s_hi) * (1.0 / 64.0)
    return x * lax.rsqrt(ms + EPS) * gain


def _inproj_kernel(x_ref, g_ref, w_ref, o_ref, h_ref):
    @pl.when(pl.program_id(1) == 0)
    def _():
        x = x_ref[...]
        ms = jnp.mean(x * x, axis=-1, keepdims=True)
        h_ref[...] = (x * lax.rsqrt(ms + EPS) * g_ref[...]).astype(BF16)

    o_ref[...] = jnp.dot(h_ref[...], w_ref[...], preferred_element_type=F32).astype(o_ref.dtype)


def _inproj(x2d, g, w):
    rows, d = x2d.shape
    tm, tn = min(1024, rows), 1024
    return pl.pallas_call(
        _inproj_kernel,
        grid=(rows // tm, ZW // tn),
        in_specs=[pl.BlockSpec((tm, d), lambda i, j: (i, 0)),
                  pl.BlockSpec((1, d), lambda i, j: (0, 0)),
                  pl.BlockSpec((d, tn), lambda i, j: (0, j))],
        out_specs=pl.BlockSpec((tm, tn), lambda i, j: (i, j)),
        out_shape=jax.ShapeDtypeStruct((rows, ZW), BF16),
        scratch_shapes=[pltpu.VMEM((tm, d), BF16)],
        compiler_params=pltpu.CompilerParams(
            dimension_semantics=("parallel", "arbitrary"), vmem_limit_bytes=VMEM_LIMIT),
        name="inproj",
    )(x2d, g, w)


def _softmax_step(s, v_tile, m_ref, l_ref, acc_ref):
    nrep = s.shape[1] // LANES
    m_old = m_ref[...]
    m_new = jnp.maximum(m_old, jnp.max(s, axis=-1, keepdims=True))
    alpha = jnp.exp(m_old - m_new)
    p = jnp.exp(s - jnp.concatenate([m_new] * nrep, axis=1))
    part = p[:, 0:LANES]
    for c in range(1, nrep):
        part = part + p[:, c * LANES:(c + 1) * LANES]
    l_ref[...] = alpha * l_ref[...] + part
    acc_ref[...] = alpha * acc_ref[...] + jnp.dot(p.astype(BF16), v_tile,
                                                  preferred_element_type=F32)
    m_ref[...] = m_new


def _row_sum(l_ref):
    return jnp.sum(l_ref[...], axis=-1, keepdims=True)


def _causal_tiles(nfull, scores_fn, value_fn, mask_fn, sa_ref, sb_ref, state):
    sa_ref[...] = scores_fn(0)

    def pair(i, carry):
        j = 2 * i
        sb_ref[...] = scores_fn(j + 1)
        _softmax_step(sa_ref[...], value_fn(j), *state)
        sa_ref[...] = scores_fn(j + 2)
        _softmax_step(sb_ref[...], value_fn(j + 1), *state)
        return carry
    lax.fori_loop(0, nfull // 2, pair, 0)
    odd = (nfull & 1) == 1

    @pl.when(odd)
    def _():
        sb_ref[...] = scores_fn(nfull)
        _softmax_step(sa_ref[...], value_fn(nfull - 1), *state)
        _softmax_step(mask_fn(sb_ref[...]), value_fn(nfull), *state)

    @pl.when(jnp.logical_not(odd))
    def _():
        _softmax_step(mask_fn(sa_ref[...]), value_fn(nfull), *state)


def _reset(m_ref, l_ref, acc_ref):
    m_ref[...] = jnp.full(m_ref.shape, NEG_INF, F32)
    l_ref[...] = jnp.zeros(l_ref.shape, F32)
    acc_ref[...] = jnp.zeros(acc_ref.shape, F32)


def _qk(q, k_tile):
    return lax.dot_general(q, k_tile, (((1,), (1,)), ((), ())), preferred_element_type=F32)


def _normalize_keys(src_ref, dst_ref, gain, lo, seq, chunk=512):
    def body(c, carry):
        r = pl.ds(pl.multiple_of(c * chunk, chunk), chunk)
        dst_ref[r, :] = _half_norm(src_ref[r, :].astype(F32), gain, lo).astype(BF16)
        return carry
    lax.fori_loop(0, seq // chunk, body, 0)


def _diff_attn_kernel(lam_ref, q_ref, k_ref, v_ref, qg_ref, kg_ref, sg_ref, o_ref,
                      kn_ref, sa_ref, sb_ref, m_ref, l_ref, acc_ref, *, tq, tk, seq, lam_init):
    qi = pl.program_id(2)
    lo = lax.broadcasted_iota(jnp.int32, (1, LANES), 1) < 64

    @pl.when(qi == 0)
    def _():
        _normalize_keys(k_ref, kn_ref, kg_ref[...], lo, seq)

    qn = _half_norm(q_ref[...].astype(F32), qg_ref[...], lo) * SCALE
    qq = jnp.concatenate([jnp.where(lo, qn, 0.0), jnp.where(lo, 0.0, qn)], axis=0).astype(BF16)

    _reset(m_ref, l_ref, acc_ref)
    q0 = qi * tq
    nfull = q0 // tk

    def rows_of(j):
        return pl.ds(pl.multiple_of(j * tk, tk), tk)

    def causal(s):
        row = lax.broadcasted_iota(jnp.int32, (2 * tq, tk), 0)
        qpos = q0 + jnp.where(row >= tq, row - tq, row)
        kpos = nfull * tk + lax.broadcasted_iota(jnp.int32, (2 * tq, tk), 1)
        return jnp.where(kpos <= qpos, s, NEG_INF)

    _causal_tiles(nfull, lambda j: _qk(qq, kn_ref[rows_of(j), :]), lambda j: v_ref[rows_of(j), :],
                  causal, sa_ref, sb_ref, (m_ref, l_ref, acc_ref))

    lp = lam_ref[...]
    lam = (jnp.exp(jnp.sum(lp[0:1] * lp[1:2], axis=-1, keepdims=True))
           - jnp.exp(jnp.sum(lp[2:3] * lp[3:4], axis=-1, keepdims=True)) + lam_init)
    acc = acc_ref[...]
    l = _row_sum(l_ref)
    o = acc[:tq] / l[:tq] - lam * (acc[tq:] / l[tq:])
    ms = jnp.mean(o * o, axis=-1, keepdims=True)
    o_ref[...] = o * lax.rsqrt(ms + EPS) * sg_ref[...] * (1.0 - lam_init)


def _diff_attn(z, lam_p, qg, kg, sg, *, batch, seq, lam_init):
    tq = min(256, seq)
    tk = min(512, seq)
    nq = seq // tq
    kern = functools.partial(_diff_attn_kernel, tq=tq, tk=tk, seq=seq, lam_init=lam_init)
    vec = lambda: pl.BlockSpec((1, LANES), lambda b, h, i: (0, 0))
    return pl.pallas_call(
        kern,
        grid=(batch, A_HEADS, nq),
        in_specs=[pl.BlockSpec((4, A_DK), lambda b, h, i: (0, 0)),
                  pl.BlockSpec((tq, LANES), lambda b, h, i: (b * nq + i, BLK_AQ + h)),
                  pl.BlockSpec((seq, LANES), lambda b, h, i: (b, BLK_AK + h)),
                  pl.BlockSpec((seq, LANES), lambda b, h, i: (b, BLK_AV + h)),
                  vec(), vec(), vec()],
        out_specs=pl.BlockSpec((tq, LANES), lambda b, h, i: (b * nq + i, h)),
        out_shape=jax.ShapeDtypeStruct((batch * seq, BRANCH_WIDTH), F32),
        scratch_shapes=[pltpu.VMEM((seq, LANES), BF16),
                        pltpu.VMEM((2 * tq, tk), F32),
                        pltpu.VMEM((2 * tq, tk), F32),
                        pltpu.VMEM((2 * tq, LANES), F32),
                        pltpu.VMEM((2 * tq, LANES), F32),
                        pltpu.VMEM((2 * tq, LANES), F32)],
        compiler_params=pltpu.CompilerParams(
            dimension_semantics=("parallel", "parallel", "arbitrary"),
            vmem_limit_bytes=VMEM_LIMIT),
        name="diff_attn",
    )(lam_p, z, z, z, qg, kg, sg)


def _compress_kernel(blk_ref, pos_ref, w1_ref, w2_ref, g_ref, o_ref):
    flat = (blk_ref[...].astype(F32) + pos_ref[...]).astype(BF16)
    h = jnp.dot(flat, w1_ref[...], preferred_element_type=F32)
    h = h * jax.nn.sigmoid(h)
    y = jnp.dot(h.astype(BF16), w2_ref[...], preferred_element_type=F32)

    @pl.when(pl.program_id(0) == 0)
    def _():
        ms = jnp.mean(y * y, axis=-1, keepdims=True)
        o_ref[...] = (y * lax.rsqrt(ms + EPS) * g_ref[...]).astype(o_ref.dtype)

    @pl.when(pl.program_id(0) != 0)
    def _():
        o_ref[...] = y.astype(o_ref.dtype)


def _compress(blocks, pos, w1, w2, gain):
    _, batch, groups, ncp, flat = blocks.shape
    sq = lambda *s: pl.BlockSpec((None,) * (5 - len(s)) + s, lambda t, b, g: (t, b, g, 0, 0))
    return pl.pallas_call(
        _compress_kernel,
        grid=(2, batch, groups),
        in_specs=[sq(ncp, flat),
                  pl.BlockSpec((None, 1, flat), lambda t, b, g: (t, 0, 0)),
                  pl.BlockSpec((None, flat, PHI_HIDDEN), lambda t, b, g: (t, 0, 0)),
                  pl.BlockSpec((None, PHI_HIDDEN, B_DK), lambda t, b, g: (t, 0, 0)),
                  pl.BlockSpec((1, B_DK), lambda t, b, g: (0, 0))],
        out_specs=sq(ncp, B_DK),
        out_shape=jax.ShapeDtypeStruct((2, batch, groups, ncp, B_DK), BF16),
        compiler_params=pltpu.CompilerParams(
            dimension_semantics=("arbitrary", "arbitrary", "arbitrary"),
            vmem_limit_bytes=VMEM_LIMIT),
        name="compress",
    )(blocks, pos, w1, w2, gain)


def _split3_bf16(x):
    hi = x.astype(BF16)
    r1 = x - hi.astype(F32)
    mid = r1.astype(BF16)
    lo = (r1 - mid.astype(F32)).astype(BF16)
    return hi, mid, lo


def _not_selected_t(val, k):
    nb, nqry = val.shape
    kf = float(k)
    sub = lax.broadcasted_iota(jnp.int32, (8, LANES), 0)

    def over_blocks(xs, op):
        acc = xs[0]
        for x in xs[1:]:
            acc = op(acc, x)
        for sh in (4, 2, 1):
            acc = op(acc, pltpu.roll(acc, sh, 0))
        return acc

    out_cols = []
    for c in range(nqry // LANES):
        vals = [val[8 * r:8 * r + 8, c * LANES:(c + 1) * LANES] for r in range(nb // 8)]
        rem = vals
        thr = jnp.full((8, LANES), -jnp.inf, F32)
        done = jnp.zeros((8, LANES), F32)
        cnt = jnp.zeros((8, LANES), F32)
        for _ in range(k):
            top = over_blocks(rem, jnp.maximum)
            hit = [x == top for x in rem]
            cnt = cnt + over_blocks([jnp.where(h, 1.0, 0.0) for h in hit], jnp.add)
            rem = [jnp.where(h, -jnp.inf, x) for h, x in zip(hit, rem)]
            thr = jnp.where(done > 0.0, thr, top)
            done = jnp.where(cnt >= kf, 1.0, done)
        above = [jnp.where(v > thr, 1.0, 0.0) for v in vals]
        need = kf - over_blocks(above, jnp.add)
        offset = jnp.zeros((8, LANES), F32)
        cols = []
        for v, ab in zip(vals, above):
            tie = jnp.where(v == thr, 1.0, 0.0)
            scan = tie
            for sh in (1, 2, 4):
                scan = scan + jnp.where(sub >= sh, pltpu.roll(scan, sh, 0), 0.0)
            keep = ab + tie * jnp.where(offset + scan <= need, 1.0, 0.0)
            cols.append(1.0 - keep)
            offset = offset + over_blocks([tie], jnp.add)
        out_cols.append(jnp.concatenate(cols, axis=0))
    return jnp.concatenate(out_cols, axis=1)


def _nsa_kernel(q_ref, kc_ref, vc_ref, ks_ref, vs_ref, kw_ref, vw_ref, gt_ref, ovt_ref,
                qg_ref, ksg_ref, kwg_ref, o_ref,
                ksn_ref, kwn_ref, sa_ref, sb_ref, m_ref, l_ref, acc_ref, *, tq, tk, seq):
    g = pl.program_id(1)
    qi = pl.program_id(2)
    ns = seq // SLC_LEN
    ncp = seq // CMP_STRIDE
    rows = B_HPG * tq
    lane = lax.broadcasted_iota(jnp.int32, (1, LANES), 1)
    lo = lane < 64
    gv = jnp.zeros((1, LANES), jnp.int32) + g
    mine = (lane >> 6) == gv
    q0 = qi * tq

    @pl.when(qi == 0)
    def _():
        _normalize_keys(kw_ref, kwn_ref, kwg_ref[...], lo, seq)
        chunk = min(512, seq)

        def body(c, carry):
            r = pl.ds(pl.multiple_of(c * chunk, chunk), chunk)
            ksn_ref[r, 0:LANES] = _half_norm(ks_ref[r, :].astype(F32), ksg_ref[...], lo).astype(BF16)
            kblk = (c * chunk + lax.broadcasted_iota(jnp.int32, (chunk, LANES), 0)) // SLC_LEN
            own = lax.broadcasted_iota(jnp.int32, (chunk, LANES), 1) == kblk
            ksn_ref[r, LANES:2 * LANES] = jnp.where(own, NEG_INF, 0.0).astype(BF16)
            return carry
        lax.fori_loop(0, seq // chunk, body, 0)

    heads = []
    for pair in range(B_HPG // 2):
        x = _half_norm(q_ref[:, pair * LANES:(pair + 1) * LANES].astype(F32), qg_ref[...], lo) * SCALE
        xr = pltpu.roll(x, 64, 1)
        for half in range(2):
            heads.append(jnp.where(mine, jnp.where(gv == half, x, xr), 0.0))
    qs = jnp.concatenate(heads, axis=0).astype(BF16)

    row = lax.broadcasted_iota(jnp.int32, (rows, 1), 0)
    qpos_col = q0 + (row & (tq - 1))

    s = _qk(qs, kc_ref[...])
    cmp_end = lax.broadcasted_iota(jnp.int32, (1, ncp), 1) * CMP_STRIDE + (CMP_LEN - 1)
    cmask = cmp_end <= qpos_col
    s = jnp.where(cmask, s, NEG_INF)
    e = jnp.where(cmask, jnp.exp(s - jnp.max(s, axis=-1, keepdims=True)), 0.0)
    den = jnp.sum(e, axis=-1, keepdims=True)
    p_cmp = e / jnp.where(den > 0.0, den, 1.0)
    o_cmp = jnp.dot(p_cmp.astype(BF16), vc_ref[...], preferred_element_type=F32)

    psum = p_cmp[0:tq]
    for h in range(1, B_HPG):
        psum = psum + p_cmp[h * tq:(h + 1) * tq]
    ovt = ovt_ref[...]
    imp_t = sum(lax.dot_general(ovt, part, (((1,), (1,)), ((), ())), preferred_element_type=F32)
                for part in _split3_bf16(psum))
    blk = lax.broadcasted_iota(jnp.int32, (ns, tq), 0)
    cur = (q0 + lax.broadcasted_iota(jnp.int32, (ns, tq), 1)) // SLC_LEN
    val = jnp.where((blk == 0) | (blk == cur) | (blk == cur - 1), BIG, imp_t)
    val = jnp.where(blk > cur, NEG_INF, val)
    notsel = _not_selected_t(val, min(SLC_TOPK, ns)).T
    if ns < LANES:
        notsel = jnp.concatenate([notsel, jnp.zeros((tq, LANES - ns), F32)], axis=1)
    qa = jnp.concatenate([qs, jnp.concatenate([notsel.astype(BF16)] * B_HPG, axis=0)], axis=1)

    _reset(m_ref, l_ref, acc_ref)
    nfull = q0 // tk

    def rows_of(j):
        return pl.ds(pl.multiple_of(j * tk, tk), tk)

    def slc_scores(j):
        return _qk(qa, ksn_ref[rows_of(j), :])

    def causal(s):
        kpos = nfull * tk + lax.broadcasted_iota(jnp.int32, (rows, tk), 1)
        return jnp.where(kpos <= qpos_col, s, NEG_INF)

    _causal_tiles(nfull, slc_scores, lambda j: vs_ref[rows_of(j), :], causal,
                  sa_ref, sb_ref, (m_ref, l_ref, acc_ref))
    o_slc = acc_ref[...] / _row_sum(l_ref)

    _reset(m_ref, l_ref, acc_ref)
    tw = tq
    nwin = WIN // tw

    def win_tile(t, kind):
        r = pl.ds(pl.multiple_of(t * tw, tw), tw)
        sc = _qk(qs, kwn_ref[r, :])
        if kind != "full":
            kl = lax.broadcasted_iota(jnp.int32, (rows, tw), 1)
            ql = lax.broadcasted_iota(jnp.int32, (rows, tw), 0) & (tq - 1)
            keep = (kl <= ql) if kind == "diag" else (kl > ql)
            sc = jnp.where(keep, sc, NEG_INF)
        _softmax_step(sc, vw_ref[r, :], m_ref, l_ref, acc_ref)

    win_tile(qi, "diag")
    for d in range(1, nwin):
        @pl.when(qi >= d)
        def _():
            win_tile(qi - d, "full")

    @pl.when(qi >= nwin)
    def _():
        win_tile(qi - nwin, "band")
    o_win = acc_ref[...] / _row_sum(l_ref)

    gates = jax.nn.sigmoid(gt_ref[...].astype(F32))

    def gate_col(br):
        cols = []
        for h in range(B_HPG):
            c = g * (B_HPG * N_NSA_BRANCH) + h * N_NSA_BRANCH + br
            cols.append(jnp.sum(jnp.where(lane == c, gates, 0.0), axis=-1, keepdims=True))
        return jnp.concatenate(cols, axis=0)

    o = gate_col(0) * o_cmp + gate_col(1) * o_slc + gate_col(2) * o_win
    for pair in range(B_HPG // 2):
        placed = []
        for half in range(2):
            oh = o[(2 * pair + half) * tq:(2 * pair + half + 1) * tq]
            placed.append(jnp.where(gv == half, oh, pltpu.roll(oh, 64, 1)))
        o_ref[:, pair * LANES:(pair + 1) * LANES] = jnp.where(lo, placed[0], placed[1])


def _nsa(z, kc, vc, ovt, qg, ksg, kwg, *, batch, seq):
    tq = min(256, seq)
    tk = min(512, seq)
    nq = seq // tq
    ns = seq // SLC_LEN
    ncp = seq // CMP_STRIDE
    rows = B_HPG * tq
    kern = functools.partial(_nsa_kernel, tq=tq, tk=tk, seq=seq)
    kv = lambda blk: pl.BlockSpec((seq, LANES), lambda b, g, i: (b, blk))
    vec = lambda: pl.BlockSpec((1, LANES), lambda b, g, i: (0, 0))
    cmp_spec = lambda: pl.BlockSpec((None, ncp, LANES), lambda b, g, i: (b, 0, 0))
    return pl.pallas_call(
        kern,
        grid=(batch, B_GROUPS, nq),
        in_specs=[pl.BlockSpec((tq, 2 * LANES), lambda b, g, i: (b * nq + i, BLK_BQ // 2 + g)),
                  cmp_spec(), cmp_spec(),
                  kv(BLK_BKV + 2), kv(BLK_BKV + 3), kv(BLK_BKV + 4), kv(BLK_BKV + 5),
                  pl.BlockSpec((tq, LANES), lambda b, g, i: (b * nq + i, BLK_BBG)),
                  pl.BlockSpec((ns, ncp), lambda b, g, i: (0, 0)),
                  vec(), vec(), vec()],
        out_specs=pl.BlockSpec((tq, 2 * LANES), lambda b, g, i: (b * nq + i, g)),
        out_shape=jax.ShapeDtypeStruct((batch * seq, BRANCH_WIDTH), F32),
        scratch_shapes=[pltpu.VMEM((seq, 2 * LANES), BF16),
                        pltpu.VMEM((seq, LANES), BF16),
                        pltpu.VMEM((rows, tk), F32),
                        pltpu.VMEM((rows, tk), F32),
                        pltpu.VMEM((rows, LANES), F32),
                        pltpu.VMEM((rows, LANES), F32),
                        pltpu.VMEM((rows, LANES), F32)],
        compiler_params=pltpu.CompilerParams(
            dimension_semantics=("parallel", "parallel", "arbitrary"),
            vmem_limit_bytes=VMEM_LIMIT),
        name="nsa",
    )(z, kc, vc, z, z, z, z, z, ovt, qg, ksg, kwg)


def _merge_kernel(x_ref, oa_ref, ob_ref, ag_ref, bg_ref, cu_ref, halo_ref, cg_ref,
                  mg0_ref, mg1_ref, mg2_ref, cw_ref, cs_ref, wb_ref, wo_ref, o_ref,
                  ext_ref, *, tm, seq):
    i = pl.program_id(0)
    tpos0 = (i * tm) % seq

    ext_ref[HALO:, :] = cu_ref[...].astype(F32)
    ext_ref[:HALO, :] = jnp.where(tpos0 == 0, 0.0, halo_ref[...].astype(F32))
    tpos = tpos0 + lax.broadcasted_iota(jnp.int32, (tm, 1), 0)
    pooled = []
    for gi, w in enumerate(C_WINDOWS):
        c = slice(gi * C_GDIM, (gi + 1) * C_GDIM)
        cur = ext_ref[HALO:, c]
        tot = cur
        for d in range(1, w):
            tot = tot + ext_ref[HALO - d:HALO - d + tm, c]
        cnt = jnp.minimum(tpos + 1, w).astype(F32)
        pg = (tot / cnt - cur).astype(BF16)
        pooled.append(jnp.dot(pg, cw_ref[gi], preferred_element_type=F32))
    oc = jnp.concatenate(pooled, axis=1) * cs_ref[...]

    def gated(o, gate_ref, k):
        gt = gate_ref[...].astype(F32)
        return jnp.dot((o * (gt * jax.nn.sigmoid(gt))).astype(BF16), wb_ref[k],
                       preferred_element_type=F32)

    merged = (jax.nn.sigmoid(mg0_ref[...].astype(F32)) * gated(oa_ref[...], ag_ref, 0)
              + jax.nn.sigmoid(mg1_ref[...].astype(F32)) * gated(ob_ref[...], bg_ref, 1)
              + jax.nn.sigmoid(mg2_ref[...].astype(F32)) * gated(oc, cg_ref, 2))
    o_ref[...] = x_ref[...] + jnp.dot(merged.astype(BF16), wo_ref[...], preferred_element_type=F32)


def _merge(x2d, z, oa, ob, cw, cs, wb, wo, *, seq):
    rows, d = x2d.shape
    tm = min(512, seq)
    bw = BRANCH_WIDTH
    kern = functools.partial(_merge_kernel, tm=tm, seq=seq)
    zcol = lambda blk: pl.BlockSpec((tm, bw), lambda i: (i, blk * LANES // bw))
    mgs = lambda k: pl.BlockSpec((tm, d), lambda i: (i, BLK_MG * LANES // d + k))
    full = lambda a: pl.BlockSpec(a.shape, lambda i: (0,) * a.ndim)
    return pl.pallas_call(
        kern,
        grid=(rows // tm,),
        in_specs=[pl.BlockSpec((tm, d), lambda i: (i, 0)),
                  pl.BlockSpec((tm, bw), lambda i: (i, 0)),
                  pl.BlockSpec((tm, bw), lambda i: (i, 0)),
                  zcol(BLK_AG), zcol(BLK_BG), zcol(BLK_CU),
                  pl.BlockSpec((HALO, bw),
                               lambda i: (jnp.maximum(i * (tm // HALO) - 1, 0), BLK_CU * LANES // bw)),
                  zcol(BLK_CG), mgs(0), mgs(1), mgs(2),
                  full(cw), full(cs), full(wb), full(wo)],
        out_specs=pl.BlockSpec((tm, d), lambda i: (i, 0)),
        out_shape=jax.ShapeDtypeStruct((rows, d), F32),
        scratch_shapes=[pltpu.VMEM((tm + HALO, bw), F32)],
        compiler_params=pltpu.CompilerParams(
            dimension_semantics=("parallel",), vmem_limit_bytes=VMEM_LIMIT),
        name="merge",
    )(x2d, oa, ob, z, z, z, z, z, z, z, z, cw, cs, wb, wo)


def _overlap_t(seq):
    ncp = seq // CMP_STRIDE
    nc = (seq - CMP_LEN) // CMP_STRIDE + 1
    ns = seq // SLC_LEN
    cs = np.arange(ncp) * CMP_STRIDE
    ce = cs + CMP_LEN
    ss = np.arange(ns) * SLC_LEN
    se = ss + SLC_LEN
    ov = np.clip(np.minimum(ce[None, :], se[:, None]) - np.maximum(cs[None, :], ss[:, None]), 0, None)
    ov = ov / CMP_LEN
    ov[:, nc:] = 0.0
    return jnp.asarray(ov, dtype=BF16)


def _cmp_blocks(z3, blk, seq):
    batch = z3.shape[0]
    ncp = seq // CMP_STRIDE
    a = z3[:, :, blk * LANES:(blk + 1) * LANES]
    a = a.reshape(batch, ncp, CMP_STRIDE, B_GROUPS, B_DK).transpose(0, 3, 1, 2, 4)
    a = a.reshape(batch, B_GROUPS, ncp, CMP_STRIDE * B_DK)
    nxt = jnp.concatenate([a[:, :, 1:], jnp.zeros_like(a[:, :, :1])], axis=2)
    return jnp.concatenate([a, nxt], axis=-1)


def kernel(x, norm_g, w_in, a_q_g, a_k_g, a_lam, a_subln_g, b_q_g, b_k_g, b_cmp_pos, b_phi_w1,
           b_phi_w2, c_w, c_scale, w_branch, w_out):
    batch, seq, d = x.shape
    x2d = x.reshape(batch * seq, d)
    ovt = _overlap_t(seq)
    two = lambda v: jnp.concatenate([v, v]).reshape(1, LANES).astype(F32)
    for l in range(DEPTH):
        z = _inproj(x2d, norm_g[l].reshape(1, d), _permute_w_in(w_in[l]).astype(BF16))
        lam_init = 0.8 - 0.6 * math.exp(-0.3 * l)
        oa = _diff_attn(z, a_lam[l].astype(F32), two(a_q_g[l]), two(a_k_g[l]),
                        a_subln_g[l].reshape(1, LANES).astype(F32),
                        batch=batch, seq=seq, lam_init=lam_init)
        z3 = z.reshape(batch, seq, ZW)
        blocks = jnp.stack([_cmp_blocks(z3, BLK_BKV, seq), _cmp_blocks(z3, BLK_BKV + 1, seq)])
        cmp = _compress(blocks, b_cmp_pos[l].reshape(2, 1, CMP_LEN * B_DK).astype(F32),
                        b_phi_w1[l].astype(BF16), b_phi_w2[l].astype(BF16),
                        b_k_g[l, 0].reshape(1, B_DK).astype(F32))
        kc = jnp.concatenate([cmp[0, :, 0], cmp[0, :, 1]], axis=-1)
        vc = jnp.concatenate([cmp[1, :, 0], cmp[1, :, 1]], axis=-1)
        ob = _nsa(z, kc, vc, ovt, two(b_q_g[l]), two(b_k_g[l, 1]), two(b_k_g[l, 2]),
                  batch=batch, seq=seq)
        x2d = _merge(x2d, z, oa, ob, c_w[l].astype(BF16), c_scale[l].reshape(1, BRANCH_WIDTH),
                     w_branch[l].astype(BF16), w_out[l].astype(BF16), seq=seq)
    return x2d.reshape(batch, seq, d)
```

```python
import functools
import math

import numpy as np
import jax
import jax.numpy as jnp
from jax import lax
from jax.experimental import pallas as pl
from jax.experimental.pallas import tpu as pltpu

F32 = jnp.float32
BF16 = jnp.bfloat16

D_MODEL = 1024
DEPTH = 2
BRANCH_WIDTH = D_MODEL // 2
A_DK = 64
A_DV = 2 * A_DK
A_HEADS = BRANCH_WIDTH // A_DV
B_DK = 64
B_HEADS = BRANCH_WIDTH // B_DK
B_GROUPS = 2
B_HPG = B_HEADS // B_GROUPS
CMP_LEN = 32
CMP_STRIDE = 16
SLC_LEN = 64
SLC_TOPK = 16
WIN = 512
PHI_HIDDEN = 256
N_NSA_BRANCH = 3
C_WINDOWS = (2, 4, 8, 16)
C_GDIM = BRANCH_WIDTH // len(C_WINDOWS)
N_BRANCH = 3
EPS = 1e-6
NEG_INF = -1e30
BIG = 1e30

LANES = 128
HALO = 16
VMEM_LIMIT = 48 * 1024 * 1024

ZW = 8192
BLK_AQ, BLK_AK, BLK_AV, BLK_AG = 0, 4, 8, 12
BLK_BQ, BLK_BG, BLK_CU, BLK_CG = 16, 20, 24, 28
BLK_MG = 32
BLK_BKV = 56
BLK_BBG = 62

SCALE = 0.125
QSCALE = SCALE * math.log2(math.e)


def _permute_w_in(w):
    o = np.cumsum([0, 512, 512, 512, 512, 512, 768, 24, 512, 512, 512, 3072])
    seg = lambda i: w[:, int(o[i]):int(o[i + 1])]
    aq, ak, av, ag, bq, bkv, bbg, bg, cu, cg, mg = [seg(i) for i in range(11)]
    parts = [aq, ak, av, ag, bq, bg, cu, cg, mg, bkv, bbg]
    used = sum(p.shape[1] for p in parts)
    parts.append(jnp.zeros((w.shape[0], ZW - used), w.dtype))
    return jnp.concatenate(parts, axis=1)


def _half_norm(x, gain, lo):
    x2 = x * x
    s_lo = jnp.sum(jnp.where(lo, x2, 0.0), axis=-1, keepdims=True)
    s_hi = jnp.sum(jnp.where(lo, 0.0, x2), axis=-1, keepdims=True)
    ms = jnp.where(lo, s_lo, s_hi) * (1.0 / 64.0)
    return x * lax.rsqrt(ms + EPS) * gain


def _inproj_kernel(x_ref, g_ref, w_ref, o_ref, h_ref):
    @pl.when(pl.program_id(1) == 0)
    def _():
        x = x_ref[...]
        ms = jnp.mean(x * x, axis=-1, keepdims=True)
        h_ref[...] = (x * lax.rsqrt(ms + EPS) * g_ref[...]).astype(BF16)

    o_ref[...] = jnp.dot(h_ref[...], w_ref[...], preferred_element_type=F32).astype(o_ref.dtype)


def _inproj(x2d, g, w):
    rows, d = x2d.shape
    tm, tn = min(1024, rows), 1024
    return pl.pallas_call(
        _inproj_kernel,
        grid=(rows // tm, ZW // tn),
        in_specs=[pl.BlockSpec((tm, d), lambda i, j: (i, 0)),
                  pl.BlockSpec((1, d), lambda i, j: (0, 0)),
                  pl.BlockSpec((d, tn), lambda i, j: (0, j))],
        out_specs=pl.BlockSpec((tm, tn), lambda i, j: (i, j)),
        out_shape=jax.ShapeDtypeStruct((rows, ZW), BF16),
        scratch_shapes=[pltpu.VMEM((tm, d), BF16)],
        compiler_params=pltpu.CompilerParams(
            dimension_semantics=("parallel", "arbitrary"), vmem_limit_bytes=VMEM_LIMIT),
        name="inproj",
    )(x2d, g, w)


def _reset(m_ref, l_ref, acc_ref):
    m_ref[...] = jnp.full(m_ref.shape, NEG_INF, F32)
    l_ref[...] = jnp.zeros(l_ref.shape, F32)
    acc_ref[...] = jnp.zeros(acc_ref.shape, F32)


def _softmax_step_t(s, vt_tile, m_ref, l_ref, acc_ref):
    m_old = m_ref[...]
    m_new = jnp.maximum(m_old, jnp.max(s, axis=0, keepdims=True))
    alpha = jnp.exp2(m_old - m_new)
    p = jnp.exp2(s - m_new)
    l_ref[...] = alpha * l_ref[...] + jnp.sum(p, axis=0, keepdims=True)
    acc_ref[...] = alpha * acc_ref[...] + jnp.dot(vt_tile, p.astype(BF16),
                                                  preferred_element_type=F32)
    m_ref[...] = m_new


def _causal_tiles_t(nfull, scores_fn, value_fn, mask_fn, sa_ref, sb_ref, state):
    sa_ref[...] = scores_fn(0)

    def pair(i, carry):
        j = 2 * i
        sb_ref[...] = scores_fn(j + 1)
        _softmax_step_t(sa_ref[...], value_fn(j), *state)
        sa_ref[...] = scores_fn(j + 2)
        _softmax_step_t(sb_ref[...], value_fn(j + 1), *state)
        return carry
    lax.fori_loop(0, nfull // 2, pair, 0)
    odd = (nfull & 1) == 1

    @pl.when(odd)
    def _():
        sb_ref[...] = scores_fn(nfull)
        _softmax_step_t(sa_ref[...], value_fn(nfull - 1), *state)
        _softmax_step_t(mask_fn(sb_ref[...]), value_fn(nfull), *state)

    @pl.when(jnp.logical_not(odd))
    def _():
        _softmax_step_t(mask_fn(sa_ref[...]), value_fn(nfull), *state)


def _transpose_values(v_ref, vt_ref, seq, chunk=512):
    chunk = min(chunk, seq)

    def body(c, carry):
        r = pl.ds(pl.multiple_of(c * chunk, chunk), chunk)
        vt_ref[:, r] = v_ref[r, :].astype(F32).T.astype(BF16)
        return carry
    lax.fori_loop(0, seq // chunk, body, 0)


def _normalize_keys(src_ref, dst_ref, gain, lo, seq, chunk=512):
    def body(c, carry):
        r = pl.ds(pl.multiple_of(c * chunk, chunk), chunk)
        dst_ref[r, :] = _half_norm(src_ref[r, :].astype(F32), gain, lo).astype(BF16)
        return carry
    lax.fori_loop(0, seq // chunk, body, 0)


def _diff_attn_kernel(lam_ref, q_ref, k_ref, v_ref, qg_ref, kg_ref, sg_ref, o_ref,
                      kn_ref, vt_ref, sa_ref, sb_ref, m_ref, l_ref, acc_ref,
                      *, tq, tk, seq, lam_init):
    qi = pl.program_id(2)
    lo = lax.broadcasted_iota(jnp.int32, (1, LANES), 1) < 64

    @pl.when(qi == 0)
    def _():
        _normalize_keys(k_ref, kn_ref, kg_ref[...], lo, seq)
        _transpose_values(v_ref, vt_ref, seq)

    qt = (_half_norm(q_ref[...].astype(F32), qg_ref[...], lo) * QSCALE).T
    top = lax.broadcasted_iota(jnp.int32, (LANES, 1), 0) < 64
    qq = jnp.concatenate([jnp.where(top, qt, 0.0), jnp.where(top, 0.0, qt)], axis=1).astype(BF16)

    _reset(m_ref, l_ref, acc_ref)
    q0 = qi * tq
    nfull = q0 // tk

    def rows_of(j):
        return pl.ds(pl.multiple_of(j * tk, tk), tk)

    def causal(s):
        col = lax.broadcasted_iota(jnp.int32, (tk, 2 * tq), 1)
        qpos = q0 + jnp.where(col >= tq, col - tq, col)
        kpos = nfull * tk + lax.broadcasted_iota(jnp.int32, (tk, 2 * tq), 0)
        return jnp.where(kpos <= qpos, s, NEG_INF)

    _causal_tiles_t(nfull,
                    lambda j: jnp.dot(kn_ref[rows_of(j), :], qq, preferred_element_type=F32),
                    lambda j: vt_ref[:, rows_of(j)],
                    causal, sa_ref, sb_ref, (m_ref, l_ref, acc_ref))

    lp = lam_ref[...]
    lam = (jnp.exp(jnp.sum(lp[0:1] * lp[1:2], axis=-1, keepdims=True))
           - jnp.exp(jnp.sum(lp[2:3] * lp[3:4], axis=-1, keepdims=True)) + lam_init)
    acc = acc_ref[...]
    l = l_ref[...]
    o = acc[:, :tq] / l[:, :tq] - lam * (acc[:, tq:] / l[:, tq:])
    ms = jnp.mean(o * o, axis=0, keepdims=True)
    o_ref[...] = (o * lax.rsqrt(ms + EPS)).T * sg_ref[...] * (1.0 - lam_init)


def _diff_attn(z, lam_p, qg, kg, sg, *, batch, seq, lam_init):
    tq = min(512, seq)
    tk = min(512, seq)
    nq = seq // tq
    kern = functools.partial(_diff_attn_kernel, tq=tq, tk=tk, seq=seq, lam_init=lam_init)
    vec = lambda: pl.BlockSpec((1, LANES), lambda b, h, i: (0, 0))
    return pl.pallas_call(
        kern,
        grid=(batch, A_HEADS, nq),
        in_specs=[pl.BlockSpec((4, A_DK), lambda b, h, i: (0, 0)),
                  pl.BlockSpec((tq, LANES), lambda b, h, i: (b * nq + i, BLK_AQ + h)),
                  pl.BlockSpec((seq, LANES), lambda b, h, i: (b, BLK_AK + h)),
                  pl.BlockSpec((seq, LANES), lambda b, h, i: (b, BLK_AV + h)),
                  vec(), vec(), vec()],
        out_specs=pl.BlockSpec((tq, LANES), lambda b, h, i: (b * nq + i, h)),
        out_shape=jax.ShapeDtypeStruct((batch * seq, BRANCH_WIDTH), F32),
        scratch_shapes=[pltpu.VMEM((seq, LANES), BF16),
                        pltpu.VMEM((LANES, seq), BF16),
                        pltpu.VMEM((tk, 2 * tq), F32),
                        pltpu.VMEM((tk, 2 * tq), F32),
                        pltpu.VMEM((1, 2 * tq), F32),
                        pltpu.VMEM((1, 2 * tq), F32),
                        pltpu.VMEM((LANES, 2 * tq), F32)],
        compiler_params=pltpu.CompilerParams(
            dimension_semantics=("parallel", "parallel", "arbitrary"),
            vmem_limit_bytes=VMEM_LIMIT),
        name="diff_attn",
    )(lam_p, z, z, z, qg, kg, sg)


def _compress_kernel(blk_ref, pos_ref, w1_ref, w2_ref, g_ref, o_ref):
    flat = (blk_ref[...].astype(F32) + pos_ref[...]).astype(BF16)
    h = jnp.dot(flat, w1_ref[...], preferred_element_type=F32)
    h = h * jax.nn.sigmoid(h)
    y = jnp.dot(h.astype(BF16), w2_ref[...], preferred_element_type=F32)

    @pl.when(pl.program_id(0) == 0)
    def _():
        ms = jnp.mean(y * y, axis=-1, keepdims=True)
        o_ref[...] = (y * lax.rsqrt(ms + EPS) * g_ref[...]).astype(o_ref.dtype)

    @pl.when(pl.program_id(0) != 0)
    def _():
        o_ref[...] = y.astype(o_ref.dtype)


def _compress(blocks, pos, w1, w2, gain):
    _, batch, groups, ncp, flat = blocks.shape
    sq = lambda *s: pl.BlockSpec((None,) * (5 - len(s)) + s, lambda t, b, g: (t, b, g, 0, 0))
    return pl.pallas_call(
        _compress_kernel,
        grid=(2, batch, groups),
        in_specs=[sq(ncp, flat),
                  pl.BlockSpec((None, 1, flat), lambda t, b, g: (t, 0, 0)),
                  pl.BlockSpec((None, flat, PHI_HIDDEN), lambda t, b, g: (t, 0, 0)),
                  pl.BlockSpec((None, PHI_HIDDEN, B_DK), lambda t, b, g: (t, 0, 0)),
                  pl.BlockSpec((1, B_DK), lambda t, b, g: (0, 0))],
        out_specs=sq(ncp, B_DK),
        out_shape=jax.ShapeDtypeStruct((2, batch, groups, ncp, B_DK), BF16),
        compiler_params=pltpu.CompilerParams(
            dimension_semantics=("arbitrary", "arbitrary", "arbitrary"),
            vmem_limit_bytes=VMEM_LIMIT),
        name="compress",
    )(blocks, pos, w1, w2, gain)


def _split3_bf16(x):
    hi = x.astype(BF16)
    r1 = x - hi.astype(F32)
    mid = r1.astype(BF16)
    lo = (r1 - mid.astype(F32)).astype(BF16)
    return hi, mid, lo


def _not_selected_t(val, k):
    nb, nqry = val.shape
    kf = float(k)
    sub = lax.broadcasted_iota(jnp.int32, (8, LANES), 0)

    def over_blocks(xs, op):
        acc = xs[0]
        for x in xs[1:]:
            acc = op(acc, x)
        for sh in (4, 2, 1):
            acc = op(acc, pltpu.roll(acc, sh, 0))
        return acc

    out_cols = []
    for c in range(nqry // LANES):
        vals = [val[8 * r:8 * r + 8, c * LANES:(c + 1) * LANES] for r in range(nb // 8)]
        rem = vals
        thr = jnp.full((8, LANES), -jnp.inf, F32)
        done = jnp.zeros((8, LANES), F32)
        cnt = jnp.zeros((8, LANES), F32)
        for _ in range(k):
            top = over_blocks(rem, jnp.maximum)
            hit = [x == top for x in rem]
            cnt = cnt + over_blocks([jnp.where(h, 1.0, 0.0) for h in hit], jnp.add)
            rem = [jnp.where(h, -jnp.inf, x) for h, x in zip(hit, rem)]
            thr = jnp.where(done > 0.0, thr, top)
            done = jnp.where(cnt >= kf, 1.0, done)
        above = [jnp.where(v > thr, 1.0, 0.0) for v in vals]
        need = kf - over_blocks(above, jnp.add)
        offset = jnp.zeros((8, LANES), F32)
        cols = []
        for v, ab in zip(vals, above):
            tie = jnp.where(v == thr, 1.0, 0.0)
            scan = tie
            for sh in (1, 2, 4):
                scan = scan + jnp.where(sub >= sh, pltpu.roll(scan, sh, 0), 0.0)
            keep = ab + tie * jnp.where(offset + scan <= need, 1.0, 0.0)
            cols.append(1.0 - keep)
            offset = offset + over_blocks([tie], jnp.add)
        out_cols.append(jnp.concatenate(cols, axis=0))
    return jnp.concatenate(out_cols, axis=1)


def _nsa_kernel(q_ref, kc_ref, vct_ref, ks_ref, vs_ref, kw_ref, vw_ref, gt_ref, ovt_ref,
                qg_ref, ksg_ref, kwg_ref, o_ref,
                ksn_ref, kwn_ref, vst_ref, vwt_ref, sa_ref, sb_ref, m_ref, l_ref, acc_ref,
                part_ref, gsel_ref, *, tq, tk, seq):
    g = pl.program_id(1)
    qi = pl.program_id(2)
    ns = seq // SLC_LEN
    ncp = seq // CMP_STRIDE
    rows = B_HPG * tq
    half = LANES // 2
    lo = lax.broadcasted_iota(jnp.int32, (1, LANES), 1) < half
    g0 = (jnp.zeros((half, tq), jnp.int32) + g) == 0
    q0 = qi * tq

    @pl.when(qi == 0)
    def _():
        _normalize_keys(kw_ref, kwn_ref, kwg_ref[...], lo, seq)
        _transpose_values(vs_ref, vst_ref, seq)
        _transpose_values(vw_ref, vwt_ref, seq)
        chunk = min(512, seq)

        def body(c, carry):
            r = pl.ds(pl.multiple_of(c * chunk, chunk), chunk)
            ksn_ref[r, 0:LANES] = _half_norm(ks_ref[r, :].astype(F32), ksg_ref[...], lo).astype(BF16)
            kblk = (c * chunk + lax.broadcasted_iota(jnp.int32, (chunk, LANES), 0)) // SLC_LEN
            own = lax.broadcasted_iota(jnp.int32, (chunk, LANES), 1) == kblk
            ksn_ref[r, LANES:2 * LANES] = jnp.where(own, NEG_INF, 0.0).astype(BF16)
            return carry
        lax.fori_loop(0, seq // chunk, body, 0)

    zeros = jnp.zeros((half, tq), F32)
    heads = []
    for pair in range(B_HPG // 2):
        xt = (_half_norm(q_ref[:, pair * LANES:(pair + 1) * LANES].astype(F32), qg_ref[...], lo)
              * QSCALE).T
        for hh in range(2):
            hrows = xt[hh * half:(hh + 1) * half]
            heads.append(jnp.concatenate([jnp.where(g0, hrows, zeros),
                                          jnp.where(g0, zeros, hrows)], axis=0))
    qs = jnp.concatenate(heads, axis=1).astype(BF16)
    qpos = q0 + (lax.broadcasted_iota(jnp.int32, (1, rows), 1) & (tq - 1))

    s = jnp.dot(kc_ref[...], qs, preferred_element_type=F32)
    cmp_end = lax.broadcasted_iota(jnp.int32, (ncp, 1), 0) * CMP_STRIDE + (CMP_LEN - 1)
    s = jnp.where(cmp_end <= qpos, s, NEG_INF)
    e = jnp.exp2(s - jnp.max(s, axis=0, keepdims=True))
    den = jnp.sum(e, axis=0, keepdims=True)
    p_cmp = e * jnp.where(qpos >= CMP_LEN - 1, 1.0 / den, 0.0)
    o_cmp = jnp.dot(vct_ref[...], p_cmp.astype(BF16), preferred_element_type=F32)

    nk = WIN + tq
    start = pl.multiple_of(jnp.maximum(q0 - WIN, 0), tq)
    kpos = start + lax.broadcasted_iota(jnp.int32, (nk, 1), 0)
    sw = jnp.dot(kwn_ref[pl.ds(start, nk), :], qs, preferred_element_type=F32)
    sw = jnp.where((kpos <= qpos) & (kpos > qpos - WIN), sw, NEG_INF)
    pw = jnp.exp2(sw - jnp.max(sw, axis=0, keepdims=True))
    o_win = (jnp.dot(vwt_ref[:, pl.ds(start, nk)], pw.astype(BF16), preferred_element_type=F32)
             / jnp.sum(pw, axis=0, keepdims=True))

    gates = jax.nn.sigmoid(gt_ref[...].astype(F32)).T
    g0row = (jnp.zeros((1, tq), jnp.int32) + g) == 0

    def gate_row(br):
        per_head = []
        for h in range(B_HPG):
            c = h * N_NSA_BRANCH + br
            c1 = B_HPG * N_NSA_BRANCH + c
            per_head.append(jnp.where(g0row, gates[c:c + 1], gates[c1:c1 + 1]))
        return jnp.concatenate(per_head, axis=1)

    part_ref[...] = gate_row(0) * o_cmp + gate_row(2) * o_win
    gsel_ref[...] = gate_row(1)

    psum = p_cmp[:, 0:tq]
    for h in range(1, B_HPG):
        psum = psum + p_cmp[:, h * tq:(h + 1) * tq]
    ovt = ovt_ref[...]
    imp_t = sum(jnp.dot(ovt, part, preferred_element_type=F32) for part in _split3_bf16(psum))
    blk = lax.broadcasted_iota(jnp.int32, (ns, tq), 0)
    cur = (q0 + lax.broadcasted_iota(jnp.int32, (ns, tq), 1)) // SLC_LEN
    val = jnp.where((blk == 0) | (blk == cur) | (blk == cur - 1), BIG, imp_t)
    val = jnp.where(blk > cur, NEG_INF, val)
    notsel = _not_selected_t(val, min(SLC_TOPK, ns))
    if ns < LANES:
        notsel = jnp.concatenate([notsel, jnp.zeros((LANES - ns, tq), F32)], axis=0)
    qa = jnp.concatenate([qs, jnp.concatenate([notsel.astype(BF16)] * B_HPG, axis=1)], axis=0)

    _reset(m_ref, l_ref, acc_ref)
    nfull = q0 // tk

    def rows_of(j):
        return pl.ds(pl.multiple_of(j * tk, tk), tk)

    def causal(sc):
        kpos = nfull * tk + lax.broadcasted_iota(jnp.int32, (tk, 1), 0)
        return jnp.where(kpos <= qpos, sc, NEG_INF)

    _causal_tiles_t(nfull,
                    lambda j: jnp.dot(ksn_ref[rows_of(j), :], qa, preferred_element_type=F32),
                    lambda j: vst_ref[:, rows_of(j)],
                    causal, sa_ref, sb_ref, (m_ref, l_ref, acc_ref))
    o = part_ref[...] + gsel_ref[...] * (acc_ref[...] / l_ref[...])
    out_t = jnp.concatenate(
        [jnp.where(g0, o[0:half, h * tq:(h + 1) * tq], o[half:LANES, h * tq:(h + 1) * tq])
         for h in range(B_HPG)], axis=0)
    o_ref[...] = out_t.T


def _nsa(z, kc, vc, ovt, qg, ksg, kwg, *, batch, seq):
    tq = min(256, seq)
    tk = min(512, seq)
    nq = seq // tq
    ns = seq // SLC_LEN
    ncp = seq // CMP_STRIDE
    rows = B_HPG * tq
    kern = functools.partial(_nsa_kernel, tq=tq, tk=tk, seq=seq)
    kv = lambda blk: pl.BlockSpec((seq, LANES), lambda b, g, i: (b, blk))
    vec = lambda: pl.BlockSpec((1, LANES), lambda b, g, i: (0, 0))
    assert seq >= WIN + tq and WIN % tq == 0 and tk % tq == 0
    return pl.pallas_call(
        kern,
        grid=(batch, B_GROUPS, nq),
        in_specs=[pl.BlockSpec((tq, 2 * LANES), lambda b, g, i: (b * nq + i, BLK_BQ // 2 + g)),
                  pl.BlockSpec((None, ncp, LANES), lambda b, g, i: (b, 0, 0)),
                  pl.BlockSpec((None, LANES, ncp), lambda b, g, i: (b, 0, 0)),
                  kv(BLK_BKV + 2), kv(BLK_BKV + 3), kv(BLK_BKV + 4), kv(BLK_BKV + 5),
                  pl.BlockSpec((tq, LANES), lambda b, g, i: (b * nq + i, BLK_BBG)),
                  pl.BlockSpec((ns, ncp), lambda b, g, i: (0, 0)),
                  vec(), vec(), vec()],
        out_specs=pl.BlockSpec((tq, 2 * LANES), lambda b, g, i: (b * nq + i, g)),
        out_shape=jax.ShapeDtypeStruct((batch * seq, BRANCH_WIDTH), F32),
        scratch_shapes=[pltpu.VMEM((seq, 2 * LANES), BF16),
                        pltpu.VMEM((seq, LANES), BF16),
                        pltpu.VMEM((LANES, seq), BF16),
                        pltpu.VMEM((LANES, seq), BF16),
                        pltpu.VMEM((tk, rows), F32),
                        pltpu.VMEM((tk, rows), F32),
                        pltpu.VMEM((1, rows), F32),
                        pltpu.VMEM((1, rows), F32),
                        pltpu.VMEM((LANES, rows), F32),
                        pltpu.VMEM((LANES, rows), F32),
                        pltpu.VMEM((1, rows), F32)],
        compiler_params=pltpu.CompilerParams(
            dimension_semantics=("parallel", "parallel", "arbitrary"),
            vmem_limit_bytes=VMEM_LIMIT),
        name="nsa",
    )(z, kc, vc, z, z, z, z, z, ovt, qg, ksg, kwg)


def _merge_kernel(x_ref, oa_ref, ob_ref, ag_ref, bg_ref, cu_ref, halo_ref, cg_ref,
                  mg0_ref, mg1_ref, mg2_ref, cw_ref, cs_ref, wb_ref, wo_ref, o_ref,
                  ext_ref, *, tm, seq):
    i = pl.program_id(0)
    tpos0 = (i * tm) % seq

    ext_ref[HALO:, :] = cu_ref[...].astype(F32)
    ext_ref[:HALO, :] = jnp.where(tpos0 == 0, 0.0, halo_ref[...].astype(F32))
    tpos = tpos0 + lax.broadcasted_iota(jnp.int32, (tm, 1), 0)
    pooled = []
    for gi, w in enumerate(C_WINDOWS):
        c = slice(gi * C_GDIM, (gi + 1) * C_GDIM)
        cur = ext_ref[HALO:, c]
        tot = cur
        for d in range(1, w):
            tot = tot + ext_ref[HALO - d:HALO - d + tm, c]
        cnt = jnp.minimum(tpos + 1, w).astype(F32)
        pg = (tot / cnt - cur).astype(BF16)
        pooled.append(jnp.dot(pg, cw_ref[gi], preferred_element_type=F32))
    oc = jnp.concatenate(pooled, axis=1) * cs_ref[...]

    def gated(o, gate_ref, k):
        gt = gate_ref[...].astype(F32)
        return jnp.dot((o * (gt * jax.nn.sigmoid(gt))).astype(BF16), wb_ref[k],
                       preferred_element_type=F32)

    merged = (jax.nn.sigmoid(mg0_ref[...].astype(F32)) * gated(oa_ref[...], ag_ref, 0)
              + jax.nn.sigmoid(mg1_ref[...].astype(F32)) * gated(ob_ref[...], bg_ref, 1)
              + jax.nn.sigmoid(mg2_ref[...].astype(F32)) * gated(oc, cg_ref, 2))
    o_ref[...] = x_ref[...] + jnp.dot(merged.astype(BF16), wo_ref[...], preferred_element_type=F32)


def _merge(x2d, z, oa, ob, cw, cs, wb, wo, *, seq):
    rows, d = x2d.shape
    tm = min(512, seq)
    bw = BRANCH_WIDTH
    kern = functools.partial(_merge_kernel, tm=tm, seq=seq)
    zcol = lambda blk: pl.BlockSpec((tm, bw), lambda i: (i, blk * LANES // bw))
    mgs = lambda k: pl.BlockSpec((tm, d), lambda i: (i, BLK_MG * LANES // d + k))
    full = lambda a: pl.BlockSpec(a.shape, lambda i: (0,) * a.ndim)
    return pl.pallas_call(
        kern,
        grid=(rows // tm,),
        in_specs=[pl.BlockSpec((tm, d), lambda i: (i, 0)),
                  pl.BlockSpec((tm, bw), lambda i: (i, 0)),
                  pl.BlockSpec((tm, bw), lambda i: (i, 0)),
                  zcol(BLK_AG), zcol(BLK_BG), zcol(BLK_CU),
                  pl.BlockSpec((HALO, bw),
                               lambda i: (jnp.maximum(i * (tm // HALO) - 1, 0), BLK_CU * LANES // bw)),
                  zcol(BLK_CG), mgs(0), mgs(1), mgs(2),
                  full(cw), full(cs), full(wb), full(wo)],
        out_specs=pl.BlockSpec((tm, d), lambda i: (i, 0)),
        out_shape=jax.ShapeDtypeStruct((rows, d), F32),
        scratch_shapes=[pltpu.VMEM((tm + HALO, bw), F32)],
        compiler_params=pltpu.CompilerParams(
            dimension_semantics=("parallel",), vmem_limit_bytes=VMEM_LIMIT),
        name="merge",
    )(x2d, oa, ob, z, z, z, z, z, z, z, z, cw, cs, wb, wo)


def _overlap_t(seq):
    ncp = seq // CMP_STRIDE
    nc = (seq - CMP_LEN) // CMP_STRIDE + 1
    ns = seq // SLC_LEN
    cs = np.arange(ncp) * CMP_STRIDE
    ce = cs + CMP_LEN
    ss = np.arange(ns) * SLC_LEN
    se = ss + SLC_LEN
    ov = np.clip(np.minimum(ce[None, :], se[:, None]) - np.maximum(cs[None, :], ss[:, None]), 0, None)
    ov = ov / CMP_LEN
    ov[:, nc:] = 0.0
    return jnp.asarray(ov, dtype=BF16)


def _cmp_blocks(z3, blk, seq):
    batch = z3.shape[0]
    ncp = seq // CMP_STRIDE
    a = z3[:, :, blk * LANES:(blk + 1) * LANES]
    a = a.reshape(batch, ncp, CMP_STRIDE, B_GROUPS, B_DK).transpose(0, 3, 1, 2, 4)
    a = a.reshape(batch, B_GROUPS, ncp, CMP_STRIDE * B_DK)
    nxt = jnp.concatenate([a[:, :, 1:], jnp.zeros_like(a[:, :, :1])], axis=2)
    return jnp.concatenate([a, nxt], axis=-1)


def kernel(x, norm_g, w_in, a_q_g, a_k_g, a_lam, a_subln_g, b_q_g, b_k_g, b_cmp_pos, b_phi_w1,
           b_phi_w2, c_w, c_scale, w_branch, w_out):
    batch, seq, d = x.shape
    x2d = x.reshape(batch * seq, d)
    ovt = _overlap_t(seq)
    two = lambda v: jnp.concatenate([v, v]).reshape(1, LANES).astype(F32)
    for l in range(DEPTH):
        z = _inproj(x2d, norm_g[l].reshape(1, d), _permute_w_in(w_in[l]).astype(BF16))
        lam_init = 0.8 - 0.6 * math.exp(-0.3 * l)
        oa = _diff_attn(z, a_lam[l].astype(F32), two(a_q_g[l]), two(a_k_g[l]),
                        a_subln_g[l].reshape(1, LANES).astype(F32),
                        batch=batch, seq=seq, lam_init=lam_init)
        z3 = z.reshape(batch, seq, ZW)
        blocks = jnp.stack([_cmp_blocks(z3, BLK_BKV, seq), _cmp_blocks(z3, BLK_BKV + 1, seq)])
        cmp = _compress(blocks, b_cmp_pos[l].reshape(2, 1, CMP_LEN * B_DK).astype(F32),
                        b_phi_w1[l].astype(BF16), b_phi_w2[l].astype(BF16),
                        b_k_g[l, 0].reshape(1, B_DK).astype(F32))
        kc = jnp.concatenate([cmp[0, :, 0], cmp[0, :, 1]], axis=-1)
        vc = jnp.concatenate([cmp[1, :, 0], cmp[1, :, 1]], axis=-1).transpose(0, 2, 1)
        ob = _nsa(z, kc, vc, ovt, two(b_q_g[l]), two(b_k_g[l, 1]), two(b_k_g[l, 2]),
                  batch=batch, seq=seq)
        x2d = _merge(x2d, z, oa, ob, c_w[l].astype(BF16), c_scale[l].reshape(1, BRANCH_WIDTH),
                     w_branch[l].astype(BF16), w_out[l].astype(BF16), seq=seq)
    return x2d.reshape(batch, seq, d)
```

```python
import functools
import math

import numpy as np
import jax
import jax.numpy as jnp
from jax import lax
from jax.experimental import pallas as pl
from jax.experimental.pallas import tpu as pltpu

F32 = jnp.float32
BF16 = jnp.bfloat16

D_MODEL = 1024
DEPTH = 2
BRANCH_WIDTH = D_MODEL // 2
A_DK = 64
A_DV = 2 * A_DK
A_HEADS = BRANCH_WIDTH // A_DV
B_DK = 64
B_HEADS = BRANCH_WIDTH // B_DK
B_GROUPS = 2
B_HPG = B_HEADS // B_GROUPS
CMP_LEN = 32
CMP_STRIDE = 16
SLC_LEN = 64
SLC_TOPK = 16
WIN = 512
PHI_HIDDEN = 256
N_NSA_BRANCH = 3
C_WINDOWS = (2, 4, 8, 16)
C_GDIM = BRANCH_WIDTH // len(C_WINDOWS)
N_BRANCH = 3
EPS = 1e-6
NEG_INF = -1e30
BIG = 1e30

LANES = 128
HALO = 16
VMEM_LIMIT = 48 * 1024 * 1024

ZW = 8192
BLK_AQ, BLK_AK, BLK_AV, BLK_AG = 0, 4, 8, 12
BLK_BQ, BLK_BG, BLK_CU, BLK_CG = 16, 20, 24, 28
BLK_MG = 32
BLK_BKV = 56
BLK_BBG = 62

SCALE = 0.125
QSCALE = SCALE * math.log2(math.e)


def _permute_w_in(w):
    o = np.cumsum([0, 512, 512, 512, 512, 512, 768, 24, 512, 512, 512, 3072])
    seg = lambda i: w[:, int(o[i]):int(o[i + 1])]
    aq, ak, av, ag, bq, bkv, bbg, bg, cu, cg, mg = [seg(i) for i in range(11)]
    parts = [aq, ak, av, ag, bq, bg, cu, cg, mg, bkv, bbg]
    used = sum(p.shape[1] for p in parts)
    parts.append(jnp.zeros((w.shape[0], ZW - used), w.dtype))
    return jnp.concatenate(parts, axis=1)


def _half_norm(x, gain, lo):
    x2 = x * x
    s_lo = jnp.sum(jnp.where(lo, x2, 0.0), axis=-1, keepdims=True)
    s_hi = jnp.sum(jnp.where(lo, 0.0, x2), axis=-1, keepdims=True)
    ms = jnp.where(lo, s_lo, s_hi) * (1.0 / 64.0)
    return x * lax.rsqrt(ms + EPS) * gain


def _inproj_kernel(x_ref, g_ref, w_ref, o_ref, h_ref):
    @pl.when(pl.program_id(1) == 0)
    def _():
        x = x_ref[...]
        ms = jnp.mean(x * x, axis=-1, keepdims=True)
        h_ref[...] = (x * lax.rsqrt(ms + EPS) * g_ref[...]).astype(BF16)

    o_ref[...] = jnp.dot(h_ref[...], w_ref[...], preferred_element_type=F32).astype(o_ref.dtype)


def _inproj(x2d, g, w):
    rows, d = x2d.shape
    tm, tn = min(1024, rows), 2048
    return pl.pallas_call(
        _inproj_kernel,
        grid=(rows // tm, ZW // tn),
        in_specs=[pl.BlockSpec((tm, d), lambda i, j: (i, 0)),
                  pl.BlockSpec((1, d), lambda i, j: (0, 0)),
                  pl.BlockSpec((d, tn), lambda i, j: (0, j))],
        out_specs=pl.BlockSpec((tm, tn), lambda i, j: (i, j)),
        out_shape=jax.ShapeDtypeStruct((rows, ZW), BF16),
        scratch_shapes=[pltpu.VMEM((tm, d), BF16)],
        compiler_params=pltpu.CompilerParams(
            dimension_semantics=("parallel", "arbitrary"), vmem_limit_bytes=VMEM_LIMIT),
        name="inproj",
    )(x2d, g, w)


def _reset(m_ref, l_ref, acc_ref):
    m_ref[...] = jnp.full(m_ref.shape, NEG_INF, F32)
    l_ref[...] = jnp.zeros(l_ref.shape, F32)
    acc_ref[...] = jnp.zeros(acc_ref.shape, F32)


def _softmax_step_t(s, vt_tile, m_ref, l_ref, acc_ref):
    m_old = m_ref[...]
    m_new = jnp.maximum(m_old, jnp.max(s, axis=0, keepdims=True))
    alpha = jnp.exp2(m_old - m_new)
    p = jnp.exp2(s - m_new)
    l_ref[...] = alpha * l_ref[...] + jnp.sum(p, axis=0, keepdims=True)
    acc_ref[...] = alpha * acc_ref[...] + jnp.dot(vt_tile, p.astype(BF16),
                                                  preferred_element_type=F32)
    m_ref[...] = m_new


def _causal_tiles_t(nfull, scores_fn, value_fn, mask_fn, sa_ref, sb_ref, state):
    sa_ref[...] = scores_fn(0)

    def two_tiles(j):
        sb_ref[...] = scores_fn(j + 1)
        _softmax_step_t(sa_ref[...], value_fn(j), *state)
        sa_ref[...] = scores_fn(j + 2)
        _softmax_step_t(sb_ref[...], value_fn(j + 1), *state)

    def quad(i, carry):
        two_tiles(4 * i)
        two_tiles(4 * i + 2)
        return carry
    nquad = nfull // 4
    lax.fori_loop(0, nquad, quad, 0)

    def pair(i, carry):
        two_tiles(4 * nquad + 2 * i)
        return carry
    lax.fori_loop(0, (nfull - 4 * nquad) // 2, pair, 0)
    odd = (nfull & 1) == 1

    @pl.when(odd)
    def _():
        sb_ref[...] = scores_fn(nfull)
        _softmax_step_t(sa_ref[...], value_fn(nfull - 1), *state)
        _softmax_step_t(mask_fn(sb_ref[...]), value_fn(nfull), *state)

    @pl.when(jnp.logical_not(odd))
    def _():
        _softmax_step_t(mask_fn(sa_ref[...]), value_fn(nfull), *state)


def _transpose_values(v_ref, vt_ref, seq, chunk=512):
    chunk = min(chunk, seq)

    def body(c, carry):
        r = pl.ds(pl.multiple_of(c * chunk, chunk), chunk)
        vt_ref[:, r] = v_ref[r, :].astype(F32).T.astype(BF16)
        return carry
    lax.fori_loop(0, seq // chunk, body, 0)


def _normalize_keys(src_ref, dst_ref, gain, lo, seq, chunk=512):
    def body(c, carry):
        r = pl.ds(pl.multiple_of(c * chunk, chunk), chunk)
        dst_ref[r, :] = _half_norm(src_ref[r, :].astype(F32), gain, lo).astype(BF16)
        return carry
    lax.fori_loop(0, seq // chunk, body, 0)


def _diff_attn_kernel(lam_ref, q_ref, k_ref, v_ref, qg_ref, kg_ref, sg_ref, o_ref,
                      kn_ref, vt_ref, sa_ref, sb_ref, m_ref, l_ref, acc_ref,
                      *, tq, tk, seq, lam_init):
    qi = pl.program_id(2)
    lo = lax.broadcasted_iota(jnp.int32, (1, LANES), 1) < 64

    @pl.when(qi == 0)
    def _():
        _normalize_keys(k_ref, kn_ref, kg_ref[...], lo, seq)
        _transpose_values(v_ref, vt_ref, seq)

    qt = (_half_norm(q_ref[...].astype(F32), qg_ref[...], lo) * QSCALE).T
    top = lax.broadcasted_iota(jnp.int32, (LANES, 1), 0) < 64
    qq = jnp.concatenate([jnp.where(top, qt, 0.0), jnp.where(top, 0.0, qt)], axis=1).astype(BF16)

    _reset(m_ref, l_ref, acc_ref)
    q0 = qi * tq
    nfull = q0 // tk

    def rows_of(j):
        return pl.ds(pl.multiple_of(j * tk, tk), tk)

    def causal(s):
        qpos = q0 + lax.broadcasted_iota(jnp.int32, (tk, tq), 1)
        kpos = nfull * tk + lax.broadcasted_iota(jnp.int32, (tk, tq), 0)
        bias = jnp.where(kpos <= qpos, 0.0, NEG_INF)
        return s + jnp.concatenate([bias, bias], axis=1)

    _causal_tiles_t(nfull,
                    lambda j: jnp.dot(kn_ref[rows_of(j), :], qq, preferred_element_type=F32),
                    lambda j: vt_ref[:, rows_of(j)],
                    causal, sa_ref, sb_ref, (m_ref, l_ref, acc_ref))

    lp = lam_ref[...]
    lam = (jnp.exp(jnp.sum(lp[0:1] * lp[1:2], axis=-1, keepdims=True))
           - jnp.exp(jnp.sum(lp[2:3] * lp[3:4], axis=-1, keepdims=True)) + lam_init)
    acc = acc_ref[...]
    l = l_ref[...]
    o = acc[:, :tq] / l[:, :tq] - lam * (acc[:, tq:] / l[:, tq:])
    ms = jnp.mean(o * o, axis=0, keepdims=True)
    o_ref[...] = (o * lax.rsqrt(ms + EPS)).T * sg_ref[...] * (1.0 - lam_init)


def _diff_attn(z, lam_p, qg, kg, sg, *, batch, seq, lam_init):
    tq = min(512, seq)
    tk = min(512, seq)
    nq = seq // tq
    kern = functools.partial(_diff_attn_kernel, tq=tq, tk=tk, seq=seq, lam_init=lam_init)
    vec = lambda: pl.BlockSpec((1, LANES), lambda b, h, i: (0, 0))
    return pl.pallas_call(
        kern,
        grid=(batch, A_HEADS, nq),
        in_specs=[pl.BlockSpec((4, A_DK), lambda b, h, i: (0, 0)),
                  pl.BlockSpec((tq, LANES), lambda b, h, i: (b * nq + i, BLK_AQ + h)),
                  pl.BlockSpec((seq, LANES), lambda b, h, i: (b, BLK_AK + h)),
                  pl.BlockSpec((seq, LANES), lambda b, h, i: (b, BLK_AV + h)),
                  vec(), vec(), vec()],
        out_specs=pl.BlockSpec((tq, LANES), lambda b, h, i: (b * nq + i, h)),
        out_shape=jax.ShapeDtypeStruct((batch * seq, BRANCH_WIDTH), F32),
        scratch_shapes=[pltpu.VMEM((seq, LANES), BF16),
                        pltpu.VMEM((LANES, seq), BF16),
                        pltpu.VMEM((tk, 2 * tq), F32),
                        pltpu.VMEM((tk, 2 * tq), F32),
                        pltpu.VMEM((1, 2 * tq), F32),
                        pltpu.VMEM((1, 2 * tq), F32),
                        pltpu.VMEM((LANES, 2 * tq), F32)],
        compiler_params=pltpu.CompilerParams(
            dimension_semantics=("parallel", "parallel", "arbitrary"),
            vmem_limit_bytes=VMEM_LIMIT),
        name="diff_attn",
    )(lam_p, z, z, z, qg, kg, sg)


def _compress_kernel(z_ref, pos_ref, w1_ref, w2_ref, g_ref, kc_ref, vct_ref, zf_ref, x_ref,
                     *, seq):
    t_kv = pl.program_id(1)
    ncp = seq // CMP_STRIDE
    lo = lax.broadcasted_iota(jnp.int32, (1, LANES), 1) < 64
    zf_ref[0:seq, :] = z_ref[...].astype(F32)
    zf_ref[seq:, :] = jnp.zeros((CMP_STRIDE, LANES), F32)
    for t in range(CMP_LEN):
        rows = zf_ref[pl.ds(t, ncp, stride=CMP_STRIDE), :]
        x_ref[:, t * LANES:(t + 1) * LANES] = (rows + pos_ref[t:t + 1, :]).astype(BF16)
    h = jnp.dot(x_ref[...], w1_ref[...], preferred_element_type=F32)
    h = h * jax.nn.sigmoid(h)
    y = jnp.dot(h.astype(BF16), w2_ref[...], preferred_element_type=F32)

    @pl.when(t_kv == 0)
    def _():
        kc_ref[...] = _half_norm(y, g_ref[...], lo).astype(BF16)

    @pl.when(t_kv != 0)
    def _():
        vct_ref[...] = y.T.astype(BF16)


def _compress(z, pos, w1, w2, gain, *, batch, seq):
    ncp = seq // CMP_STRIDE
    wide = CMP_LEN * LANES
    return pl.pallas_call(
        functools.partial(_compress_kernel, seq=seq),
        grid=(batch, 2),
        in_specs=[pl.BlockSpec((seq, LANES), lambda b, t: (b, BLK_BKV + t)),
                  pl.BlockSpec((None, CMP_LEN, LANES), lambda b, t: (t, 0, 0)),
                  pl.BlockSpec((None, wide, 2 * PHI_HIDDEN), lambda b, t: (t, 0, 0)),
                  pl.BlockSpec((None, 2 * PHI_HIDDEN, LANES), lambda b, t: (t, 0, 0)),
                  pl.BlockSpec((1, LANES), lambda b, t: (0, 0))],
        out_specs=[pl.BlockSpec((None, ncp, LANES), lambda b, t: (b, 0, 0)),
                   pl.BlockSpec((None, LANES, ncp), lambda b, t: (b, 0, 0))],
        out_shape=[jax.ShapeDtypeStruct((batch, ncp, LANES), BF16),
                   jax.ShapeDtypeStruct((batch, LANES, ncp), BF16)],
        scratch_shapes=[pltpu.VMEM((seq + CMP_STRIDE, LANES), F32),
                        pltpu.VMEM((ncp, wide), BF16)],
        compiler_params=pltpu.CompilerParams(
            dimension_semantics=("arbitrary", "arbitrary"), vmem_limit_bytes=VMEM_LIMIT),
        name="compress",
    )(z, pos, w1, w2, gain)


def _compress_weights(pos, w1, w2):
    pos2 = jnp.concatenate([pos, pos], axis=-1).astype(F32)
    w1r = w1.reshape(2, CMP_LEN, B_DK, PHI_HIDDEN)
    z1 = jnp.zeros_like(w1r)
    w1b = jnp.concatenate([jnp.concatenate([w1r, z1], axis=-1),
                           jnp.concatenate([z1, w1r], axis=-1)], axis=2)
    z2 = jnp.zeros_like(w2)
    w2b = jnp.concatenate([jnp.concatenate([w2, z2], axis=-1),
                           jnp.concatenate([z2, w2], axis=-1)], axis=1)
    return pos2, w1b.reshape(2, CMP_LEN * LANES, 2 * PHI_HIDDEN).astype(BF16), w2b.astype(BF16)


def _split3_bf16(x):
    hi = x.astype(BF16)
    r1 = x - hi.astype(F32)
    mid = r1.astype(BF16)
    lo = (r1 - mid.astype(F32)).astype(BF16)
    return hi, mid, lo


def _not_selected_t(val, k):
    nb, nqry = val.shape
    kf = float(k)
    sub = lax.broadcasted_iota(jnp.int32, (8, LANES), 0)

    def over_blocks(xs, op):
        acc = xs[0]
        for x in xs[1:]:
            acc = op(acc, x)
        for sh in (4, 2, 1):
            acc = op(acc, pltpu.roll(acc, sh, 0))
        return acc

    out_cols = []
    for c in range(nqry // LANES):
        vals = [val[8 * r:8 * r + 8, c * LANES:(c + 1) * LANES] for r in range(nb // 8)]
        rem = vals
        thr = jnp.full((8, LANES), -jnp.inf, F32)
        done = jnp.zeros((8, LANES), F32)
        cnt = jnp.zeros((8, LANES), F32)
        for _ in range(k):
            top = over_blocks(rem, jnp.maximum)
            hit = [x == top for x in rem]
            cnt = cnt + over_blocks([jnp.where(h, 1.0, 0.0) for h in hit], jnp.add)
            rem = [jnp.where(h, -jnp.inf, x) for h, x in zip(hit, rem)]
            thr = jnp.where(done > 0.0, thr, top)
            done = jnp.where(cnt >= kf, 1.0, done)
        above = [jnp.where(v > thr, 1.0, 0.0) for v in vals]
        need = kf - over_blocks(above, jnp.add)
        offset = jnp.zeros((8, LANES), F32)
        cols = []
        for v, ab in zip(vals, above):
            tie = jnp.where(v == thr, 1.0, 0.0)
            scan = tie
            for sh in (1, 2, 4):
                scan = scan + jnp.where(sub >= sh, pltpu.roll(scan, sh, 0), 0.0)
            keep = ab + tie * jnp.where(offset + scan <= need, 1.0, 0.0)
            cols.append(1.0 - keep)
            offset = offset + over_blocks([tie], jnp.add)
        out_cols.append(jnp.concatenate(cols, axis=0))
    return jnp.concatenate(out_cols, axis=1)


def _nsa_kernel(q_ref, kc_ref, vct_ref, ks_ref, vs_ref, kw_ref, vw_ref, gt_ref, ovt_ref,
                qg_ref, ksg_ref, kwg_ref, o_ref,
                ksn_ref, kwn_ref, vst_ref, vwt_ref, sa_ref, sb_ref, m_ref, l_ref, acc_ref,
                part_ref, gsel_ref, *, tq, tk, seq):
    qi = pl.program_id(1)
    g = pl.program_id(2)
    ns = seq // SLC_LEN
    ncp = seq // CMP_STRIDE
    rows = B_HPG * tq
    half = LANES // 2
    lo = lax.broadcasted_iota(jnp.int32, (1, LANES), 1) < half
    g0 = (jnp.zeros((half, tq), jnp.int32) + g) == 0
    q0 = qi * tq

    @pl.when((qi == 0) & (g == 0))
    def _():
        _normalize_keys(kw_ref, kwn_ref, kwg_ref[...], lo, seq)
        _transpose_values(vs_ref, vst_ref, seq)
        _transpose_values(vw_ref, vwt_ref, seq)
        chunk = min(512, seq)

        def body(c, carry):
            r = pl.ds(pl.multiple_of(c * chunk, chunk), chunk)
            ksn_ref[r, 0:LANES] = _half_norm(ks_ref[r, :].astype(F32), ksg_ref[...], lo).astype(BF16)
            kblk = (c * chunk + lax.broadcasted_iota(jnp.int32, (chunk, LANES), 0)) // SLC_LEN
            own = lax.broadcasted_iota(jnp.int32, (chunk, LANES), 1) == kblk
            ksn_ref[r, LANES:2 * LANES] = jnp.where(own, NEG_INF, 0.0).astype(BF16)
            return carry
        lax.fori_loop(0, seq // chunk, body, 0)

    zeros = jnp.zeros((half, tq), F32)
    heads = []
    for pair in range(B_HPG // 2):
        xt = (_half_norm(q_ref[:, pair * LANES:(pair + 1) * LANES].astype(F32), qg_ref[...], lo)
              * QSCALE).T
        for hh in range(2):
            hrows = xt[hh * half:(hh + 1) * half]
            heads.append(jnp.concatenate([jnp.where(g0, hrows, zeros),
                                          jnp.where(g0, zeros, hrows)], axis=0))
    qs = jnp.concatenate(heads, axis=1).astype(BF16)
    qpos = q0 + (lax.broadcasted_iota(jnp.int32, (1, rows), 1) & (tq - 1))
    qpos1 = q0 + lax.broadcasted_iota(jnp.int32, (1, tq), 1)

    def per_head(bias):
        return jnp.concatenate([bias] * B_HPG, axis=1)

    s = jnp.dot(kc_ref[...], qs, preferred_element_type=F32)
    cmp_end = lax.broadcasted_iota(jnp.int32, (ncp, 1), 0) * CMP_STRIDE + (CMP_LEN - 1)
    s = s + per_head(jnp.where(cmp_end <= qpos1, 0.0, NEG_INF))
    e = jnp.exp2(s - jnp.max(s, axis=0, keepdims=True))
    den = jnp.sum(e, axis=0, keepdims=True)
    p_cmp = e * jnp.where(qpos >= CMP_LEN - 1, 1.0 / den, 0.0)
    o_cmp = jnp.dot(vct_ref[...], p_cmp.astype(BF16), preferred_element_type=F32)

    nk = WIN + tq
    start = pl.multiple_of(jnp.maximum(q0 - WIN, 0), tq)
    kpos = start + lax.broadcasted_iota(jnp.int32, (nk, 1), 0)
    sw = jnp.dot(kwn_ref[pl.ds(start, nk), :], qs, preferred_element_type=F32)
    sw = sw + per_head(jnp.where((kpos <= qpos1) & (kpos > qpos1 - WIN), 0.0, NEG_INF))
    pw = jnp.exp2(sw - jnp.max(sw, axis=0, keepdims=True))
    o_win = (jnp.dot(vwt_ref[:, pl.ds(start, nk)], pw.astype(BF16), preferred_element_type=F32)
             / jnp.sum(pw, axis=0, keepdims=True))

    gates = jax.nn.sigmoid(gt_ref[...].astype(F32)).T
    g0row = (jnp.zeros((1, tq), jnp.int32) + g) == 0

    def gate_row(br):
        per_head = []
        for h in range(B_HPG):
            c = h * N_NSA_BRANCH + br
            c1 = B_HPG * N_NSA_BRANCH + c
            per_head.append(jnp.where(g0row, gates[c:c + 1], gates[c1:c1 + 1]))
        return jnp.concatenate(per_head, axis=1)

    part_ref[...] = gate_row(0) * o_cmp + gate_row(2) * o_win
    gsel_ref[...] = gate_row(1)

    psum = p_cmp[:, 0:tq]
    for h in range(1, B_HPG):
        psum = psum + p_cmp[:, h * tq:(h + 1) * tq]
    ovt = ovt_ref[...]
    imp_t = sum(jnp.dot(ovt, part, preferred_element_type=F32) for part in _split3_bf16(psum))
    blk = lax.broadcasted_iota(jnp.int32, (ns, tq), 0)
    cur = (q0 + lax.broadcasted_iota(jnp.int32, (ns, tq), 1)) // SLC_LEN
    val = jnp.where((blk == 0) | (blk == cur) | (blk == cur - 1), BIG, imp_t)
    val = jnp.where(blk > cur, NEG_INF, val)
    notsel = _not_selected_t(val, min(SLC_TOPK, ns))
    if ns < LANES:
        notsel = jnp.concatenate([notsel, jnp.zeros((LANES - ns, tq), F32)], axis=0)
    qa = jnp.concatenate([qs, jnp.concatenate([notsel.astype(BF16)] * B_HPG, axis=1)], axis=0)

    _reset(m_ref, l_ref, acc_ref)
    nfull = q0 // tk

    def rows_of(j):
        return pl.ds(pl.multiple_of(j * tk, tk), tk)

    def causal(sc):
        kpos = nfull * tk + lax.broadcasted_iota(jnp.int32, (tk, 1), 0)
        return sc + per_head(jnp.where(kpos <= qpos1, 0.0, NEG_INF))

    _causal_tiles_t(nfull,
                    lambda j: jnp.dot(ksn_ref[rows_of(j), :], qa, preferred_element_type=F32),
                    lambda j: vst_ref[:, rows_of(j)],
                    causal, sa_ref, sb_ref, (m_ref, l_ref, acc_ref))
    o = part_ref[...] + gsel_ref[...] * (acc_ref[...] / l_ref[...])
    out_t = jnp.concatenate(
        [jnp.where(g0, o[0:half, h * tq:(h + 1) * tq], o[half:LANES, h * tq:(h + 1) * tq])
         for h in range(B_HPG)], axis=0)
    o_ref[...] = out_t.T


def _nsa(z, kc, vc, ovt, qg, ksg, kwg, *, batch, seq):
    tq = min(256, seq)
    tk = min(512, seq)
    nq = seq // tq
    ns = seq // SLC_LEN
    ncp = seq // CMP_STRIDE
    rows = B_HPG * tq
    kern = functools.partial(_nsa_kernel, tq=tq, tk=tk, seq=seq)
    kv = lambda blk: pl.BlockSpec((seq, LANES), lambda b, i, g: (b, blk))
    vec = lambda: pl.BlockSpec((1, LANES), lambda b, i, g: (0, 0))
    assert seq >= WIN + tq and WIN % tq == 0 and tk % tq == 0
    return pl.pallas_call(
        kern,
        grid=(batch, nq, B_GROUPS),
        in_specs=[pl.BlockSpec((tq, 2 * LANES), lambda b, i, g: (b * nq + i, BLK_BQ // 2 + g)),
                  pl.BlockSpec((None, ncp, LANES), lambda b, i, g: (b, 0, 0)),
                  pl.BlockSpec((None, LANES, ncp), lambda b, i, g: (b, 0, 0)),
                  kv(BLK_BKV + 2), kv(BLK_BKV + 3), kv(BLK_BKV + 4), kv(BLK_BKV + 5),
                  pl.BlockSpec((tq, LANES), lambda b, i, g: (b * nq + i, BLK_BBG)),
                  pl.BlockSpec((ns, ncp), lambda b, i, g: (0, 0)),
                  vec(), vec(), vec()],
        out_specs=pl.BlockSpec((tq, 2 * LANES), lambda b, i, g: (b * nq + i, g)),
        out_shape=jax.ShapeDtypeStruct((batch * seq, BRANCH_WIDTH), F32),
        scratch_shapes=[pltpu.VMEM((seq, 2 * LANES), BF16),
                        pltpu.VMEM((seq, LANES), BF16),
                        pltpu.VMEM((LANES, seq), BF16),
                        pltpu.VMEM((LANES, seq), BF16),
                        pltpu.VMEM((tk, rows), F32),
                        pltpu.VMEM((tk, rows), F32),
                        pltpu.VMEM((1, rows), F32),
                        pltpu.VMEM((1, rows), F32),
                        pltpu.VMEM((LANES, rows), F32),
                        pltpu.VMEM((LANES, rows), F32),
                        pltpu.VMEM((1, rows), F32)],
        compiler_params=pltpu.CompilerParams(
            dimension_semantics=("parallel", "arbitrary", "arbitrary"),
            vmem_limit_bytes=VMEM_LIMIT),
        name="nsa",
    )(z, kc, vc, z, z, z, z, z, ovt, qg, ksg, kwg)


def _merge_kernel(x_ref, oa_ref, ob_ref, ag_ref, bg_ref, cu_ref, halo_ref, cg_ref,
                  mg0_ref, mg1_ref, mg2_ref, cw_ref, cs_ref, wb_ref, wo_ref, o_ref,
                  ext_ref, *, tm, seq):
    i = pl.program_id(0)
    tpos0 = (i * tm) % seq

    ext_ref[HALO:, :] = cu_ref[...].astype(F32)
    ext_ref[:HALO, :] = jnp.where(tpos0 == 0, 0.0, halo_ref[...].astype(F32))
    tpos = tpos0 + lax.broadcasted_iota(jnp.int32, (tm, 1), 0)
    pooled = []
    for gi, w in enumerate(C_WINDOWS):
        c = slice(gi * C_GDIM, (gi + 1) * C_GDIM)
        cur = ext_ref[HALO:, c]
        tot = cur
        for d in range(1, w):
            tot = tot + ext_ref[HALO - d:HALO - d + tm, c]
        cnt = jnp.minimum(tpos + 1, w).astype(F32)
        pg = (tot / cnt - cur).astype(BF16)
        pooled.append(jnp.dot(pg, cw_ref[gi], preferred_element_type=F32))
    oc = jnp.concatenate(pooled, axis=1) * cs_ref[...]

    def gated(o, gate_ref, k):
        gt = gate_ref[...].astype(F32)
        return jnp.dot((o * (gt * jax.nn.sigmoid(gt))).astype(BF16), wb_ref[k],
                       preferred_element_type=F32)

    merged = (jax.nn.sigmoid(mg0_ref[...].astype(F32)) * gated(oa_ref[...], ag_ref, 0)
              + jax.nn.sigmoid(mg1_ref[...].astype(F32)) * gated(ob_ref[...], bg_ref, 1)
              + jax.nn.sigmoid(mg2_ref[...].astype(F32)) * gated(oc, cg_ref, 2))
    o_ref[...] = x_ref[...] + jnp.dot(merged.astype(BF16), wo_ref[...], preferred_element_type=F32)


def _merge(x2d, z, oa, ob, cw, cs, wb, wo, *, seq):
    rows, d = x2d.shape
    tm = min(512, seq)
    bw = BRANCH_WIDTH
    kern = functools.partial(_merge_kernel, tm=tm, seq=seq)
    zcol = lambda blk: pl.BlockSpec((tm, bw), lambda i: (i, blk * LANES // bw))
    mgs = lambda k: pl.BlockSpec((tm, d), lambda i: (i, BLK_MG * LANES // d + k))
    full = lambda a: pl.BlockSpec(a.shape, lambda i: (0,) * a.ndim)
    return pl.pallas_call(
        kern,
        grid=(rows // tm,),
        in_specs=[pl.BlockSpec((tm, d), lambda i: (i, 0)),
                  pl.BlockSpec((tm, bw), lambda i: (i, 0)),
                  pl.BlockSpec((tm, bw), lambda i: (i, 0)),
                  zcol(BLK_AG), zcol(BLK_BG), zcol(BLK_CU),
                  pl.BlockSpec((HALO, bw),
                               lambda i: (jnp.maximum(i * (tm // HALO) - 1, 0), BLK_CU * LANES // bw)),
                  zcol(BLK_CG), mgs(0), mgs(1), mgs(2),
                  full(cw), full(cs), full(wb), full(wo)],
        out_specs=pl.BlockSpec((tm, d), lambda i: (i, 0)),
        out_shape=jax.ShapeDtypeStruct((rows, d), F32),
        scratch_shapes=[pltpu.VMEM((tm + HALO, bw), F32)],
        compiler_params=pltpu.CompilerParams(
            dimension_semantics=("parallel",), vmem_limit_bytes=VMEM_LIMIT),
        name="merge",
    )(x2d, oa, ob, z, z, z, z, z, z, z, z, cw, cs, wb, wo)


def _overlap_t(seq):
    ncp = seq // CMP_STRIDE
    nc = (seq - CMP_LEN) // CMP_STRIDE + 1
    ns = seq // SLC_LEN
    cs = np.arange(ncp) * CMP_STRIDE
    ce = cs + CMP_LEN
    ss = np.arange(ns) * SLC_LEN
    se = ss + SLC_LEN
    ov = np.clip(np.minimum(ce[None, :], se[:, None]) - np.maximum(cs[None, :], ss[:, None]), 0, None)
    ov = ov / CMP_LEN
    ov[:, nc:] = 0.0
    return jnp.asarray(ov, dtype=BF16)


def kernel(x, norm_g, w_in, a_q_g, a_k_g, a_lam, a_subln_g, b_q_g, b_k_g, b_cmp_pos, b_phi_w1,
           b_phi_w2, c_w, c_scale, w_branch, w_out):
    batch, seq, d = x.shape
    x2d = x.reshape(batch * seq, d)
    ovt = _overlap_t(seq)
    two = lambda v: jnp.concatenate([v, v]).reshape(1, LANES).astype(F32)
    for l in range(DEPTH):
        z = _inproj(x2d, norm_g[l].reshape(1, d), _permute_w_in(w_in[l].astype(BF16)))
        lam_init = 0.8 - 0.6 * math.exp(-0.3 * l)
        oa = _diff_attn(z, a_lam[l].astype(F32), two(a_q_g[l]), two(a_k_g[l]),
                        a_subln_g[l].reshape(1, LANES).astype(F32),
                        batch=batch, seq=seq, lam_init=lam_init)
        kc, vc = _compress(z, *_compress_weights(b_cmp_pos[l], b_phi_w1[l], b_phi_w2[l]),
                           two(b_k_g[l, 0]), batch=batch, seq=seq)
        ob = _nsa(z, kc, vc, ovt, two(b_q_g[l]), two(b_k_g[l, 1]), two(b_k_g[l, 2]),
                  batch=batch, seq=seq)
        x2d = _merge(x2d, z, oa, ob, c_w[l].astype(BF16), c_scale[l].reshape(1, BRANCH_WIDTH),
                     w_branch[l].astype(BF16), w_out[l].astype(BF16), seq=seq)
    return x2d.reshape(batch, seq, d)
```

```python
import functools
import math

import numpy as np
import jax
import jax.numpy as jnp
from jax import lax
from jax.experimental import pallas as pl
from jax.experimental.pallas import tpu as pltpu

F32 = jnp.float32
BF16 = jnp.bfloat16

D_MODEL = 1024
DEPTH = 2
BRANCH_WIDTH = D_MODEL // 2
A_DK = 64
A_DV = 2 * A_DK
A_HEADS = BRANCH_WIDTH // A_DV
B_DK = 64
B_HEADS = BRANCH_WIDTH // B_DK
B_GROUPS = 2
B_HPG = B_HEADS // B_GROUPS
CMP_LEN = 32
CMP_STRIDE = 16
SLC_LEN = 64
SLC_TOPK = 16
WIN = 512
PHI_HIDDEN = 256
N_NSA_BRANCH = 3
C_WINDOWS = (2, 4, 8, 16)
C_GDIM = BRANCH_WIDTH // len(C_WINDOWS)
N_BRANCH = 3
EPS = 1e-6
NEG_INF = -1e30
BIG = 1e30

LANES = 128
HALO = 16
ONES_ROWS = 16
VMEM_LIMIT = 48 * 1024 * 1024

ZW = 8192
BLK_AQ, BLK_AK, BLK_AV, BLK_AG = 0, 4, 8, 12
BLK_BQ, BLK_BG, BLK_CU, BLK_CG = 16, 20, 24, 28
BLK_MG = 32
BLK_BKV = 56
BLK_BBG = 62

SCALE = 0.125
QSCALE = SCALE * math.log2(math.e)


def _permute_w_in(w):
    o = np.cumsum([0, 512, 512, 512, 512, 512, 768, 24, 512, 512, 512, 3072])
    seg = lambda i: w[:, int(o[i]):int(o[i + 1])]
    aq, ak, av, ag, bq, bkv, bbg, bg, cu, cg, mg = [seg(i) for i in range(11)]
    parts = [aq, ak, av, ag, bq, bg, cu, cg, mg, bkv, bbg]
    used = sum(p.shape[1] for p in parts)
    parts.append(jnp.zeros((w.shape[0], ZW - used), w.dtype))
    return jnp.concatenate(parts, axis=1)


def _sigmoid(x):
    return 0.5 * jnp.tanh(0.5 * x) + 0.5


def _half_norm(x, gain, lo):
    x2 = x * x
    s_lo = jnp.sum(jnp.where(lo, x2, 0.0), axis=-1, keepdims=True)
    s_hi = jnp.sum(jnp.where(lo, 0.0, x2), axis=-1, keepdims=True)
    ms = jnp.where(lo, s_lo, s_hi) * (1.0 / 64.0)
    return x * lax.rsqrt(ms + EPS) * gain


def _inproj_kernel(x_ref, g_ref, w_ref, o_ref, h_ref):
    @pl.when(pl.program_id(1) == 0)
    def _():
        x = x_ref[...]
        ms = jnp.mean(x * x, axis=-1, keepdims=True)
        h_ref[...] = (x * lax.rsqrt(ms + EPS) * g_ref[...]).astype(BF16)

    o_ref[...] = jnp.dot(h_ref[...], w_ref[...], preferred_element_type=F32).astype(o_ref.dtype)


def _inproj(x2d, g, w):
    rows, d = x2d.shape
    tm, tn = min(1024, rows), 2048
    return pl.pallas_call(
        _inproj_kernel,
        grid=(rows // tm, ZW // tn),
        in_specs=[pl.BlockSpec((tm, d), lambda i, j: (i, 0)),
                  pl.BlockSpec((1, d), lambda i, j: (0, 0)),
                  pl.BlockSpec((d, tn), lambda i, j: (0, j))],
        out_specs=pl.BlockSpec((tm, tn), lambda i, j: (i, j)),
        out_shape=jax.ShapeDtypeStruct((rows, ZW), BF16),
        scratch_shapes=[pltpu.VMEM((tm, d), BF16)],
        compiler_params=pltpu.CompilerParams(
            dimension_semantics=("parallel", "arbitrary"), vmem_limit_bytes=VMEM_LIMIT),
        name="inproj",
    )(x2d, g, w)


def _reset(m_ref, l_ref, acc_ref):
    m_ref[...] = jnp.full(m_ref.shape, NEG_INF, F32)
    l_ref[...] = jnp.zeros(l_ref.shape, F32)
    acc_ref[...] = jnp.zeros(acc_ref.shape, F32)


def _softmax_step_t(s, vt_tile, m_ref, l_ref, acc_ref):
    m_old = m_ref[...]
    m_new = jnp.maximum(m_old, jnp.max(s, axis=0, keepdims=True))
    alpha = jnp.exp2(m_old - m_new)
    p = jnp.exp2(s - m_new)
    l_ref[...] = alpha * l_ref[...] + jnp.sum(p, axis=0, keepdims=True)
    acc_ref[...] = alpha * acc_ref[...] + jnp.dot(vt_tile, p.astype(BF16),
                                                  preferred_element_type=F32)
    m_ref[...] = m_new


def _causal_tiles_t(nfull, chains):
    def issue(j, into_b):
        for scores_fn, _, _, sa_ref, sb_ref, _ in chains:
            (sb_ref if into_b else sa_ref)[...] = scores_fn(j)

    def consume(j, from_b, masked=False):
        for _, value_fn, mask_fn, sa_ref, sb_ref, state in chains:
            s = (sb_ref if from_b else sa_ref)[...]
            _softmax_step_t(mask_fn(s) if masked else s, value_fn(j), *state)

    def two_tiles(j):
        issue(j + 1, True)
        consume(j, False)
        issue(j + 2, False)
        consume(j + 1, True)

    issue(0, False)

    def quad(i, carry):
        two_tiles(4 * i)
        two_tiles(4 * i + 2)
        return carry
    done = 4 * (nfull // 4)
    lax.fori_loop(0, nfull // 4, quad, 0)

    def pair(i, carry):
        two_tiles(done + 2 * i)
        return carry
    lax.fori_loop(0, (nfull - done) // 2, pair, 0)
    odd = (nfull & 1) == 1

    @pl.when(odd)
    def _():
        issue(nfull, True)
        consume(nfull - 1, False)
        consume(nfull, True, masked=True)

    @pl.when(jnp.logical_not(odd))
    def _():
        consume(nfull, False, masked=True)


def _transpose_values(v_ref, vt_ref, seq, chunk=512):
    chunk = min(chunk, seq)

    def body(c, carry):
        r = pl.ds(pl.multiple_of(c * chunk, chunk), chunk)
        vt_ref[:, r] = v_ref[r, :].astype(F32).T.astype(BF16)
        return carry
    lax.fori_loop(0, seq // chunk, body, 0)


def _normalize_keys(src_ref, dst_ref, gain, lo, seq, chunk=512):
    def body(c, carry):
        r = pl.ds(pl.multiple_of(c * chunk, chunk), chunk)
        dst_ref[r, :] = _half_norm(src_ref[r, :].astype(F32), gain, lo).astype(BF16)
        return carry
    lax.fori_loop(0, seq // chunk, body, 0)


A_HEADS_PER_STEP = 1


def _diff_attn_kernel(lam_ref, q_ref, k_ref, v_ref, qg_ref, kg_ref, sg_ref, o_ref, *scratch,
                      tq, tk, seq, lam_init):
    qi = pl.program_id(2)
    lo = lax.broadcasted_iota(jnp.int32, (1, LANES), 1) < 64
    per_head = [scratch[7 * h:7 * h + 7] for h in range(A_HEADS_PER_STEP)]

    @pl.when(qi == 0)
    def _():
        for h, (kn_ref, vt_ref, *_) in enumerate(per_head):
            lanes = slice(h * LANES, (h + 1) * LANES)
            _normalize_keys(k_ref.at[:, lanes], kn_ref, kg_ref[...], lo, seq)
            _transpose_values(v_ref.at[:, lanes], vt_ref, seq)

    q0 = qi * tq
    nfull = q0 // tk
    top = lax.broadcasted_iota(jnp.int32, (LANES, 1), 0) < 64

    def rows_of(j):
        return pl.ds(pl.multiple_of(j * tk, tk), tk)

    def causal(s):
        qpos = q0 + lax.broadcasted_iota(jnp.int32, (tk, tq), 1)
        kpos = nfull * tk + lax.broadcasted_iota(jnp.int32, (tk, tq), 0)
        bias = jnp.where(kpos <= qpos, 0.0, NEG_INF)
        return s + jnp.concatenate([bias, bias], axis=1)

    def chain(h):
        kn_ref, vt_ref, sa_ref, sb_ref, m_ref, l_ref, acc_ref = per_head[h]
        q = q_ref[:, h * LANES:(h + 1) * LANES].astype(F32)
        qt = (_half_norm(q, qg_ref[...], lo) * QSCALE).T
        qq = jnp.concatenate([jnp.where(top, qt, 0.0), jnp.where(top, 0.0, qt)],
                             axis=1).astype(BF16)
        _reset(m_ref, l_ref, acc_ref)
        return (lambda j: jnp.dot(kn_ref[rows_of(j), :], qq, preferred_element_type=F32),
                lambda j: vt_ref[:, rows_of(j)],
                causal, sa_ref, sb_ref, (m_ref, l_ref, acc_ref))

    _causal_tiles_t(nfull, [chain(h) for h in range(A_HEADS_PER_STEP)])

    lp = lam_ref[...]
    lam = (jnp.exp(jnp.sum(lp[0:1] * lp[1:2], axis=-1, keepdims=True))
           - jnp.exp(jnp.sum(lp[2:3] * lp[3:4], axis=-1, keepdims=True)) + lam_init)
    for h, (_, _, _, _, _, l_ref, acc_ref) in enumerate(per_head):
        acc = acc_ref[...]
        l = l_ref[...]
        o = acc[:, :tq] / l[:, :tq] - lam * (acc[:, tq:] / l[:, tq:])
        ms = jnp.mean(o * o, axis=0, keepdims=True)
        o_ref[:, h * LANES:(h + 1) * LANES] = ((o * lax.rsqrt(ms + EPS)).T * sg_ref[...]
                                               * (1.0 - lam_init))


def _diff_attn(z, lam_p, qg, kg, sg, *, batch, seq, lam_init):
    tq = min(512, seq)
    tk = min(512, seq)
    nq = seq // tq
    hps = A_HEADS_PER_STEP
    wide = hps * LANES
    kern = functools.partial(_diff_attn_kernel, tq=tq, tk=tk, seq=seq, lam_init=lam_init)
    vec = lambda: pl.BlockSpec((1, LANES), lambda b, h, i: (0, 0))
    head_scratch = [pltpu.VMEM((seq, LANES), BF16),
                    pltpu.VMEM((LANES, seq), BF16),
                    pltpu.VMEM((tk, 2 * tq), F32),
                    pltpu.VMEM((tk, 2 * tq), F32),
                    pltpu.VMEM((1, 2 * tq), F32),
                    pltpu.VMEM((1, 2 * tq), F32),
                    pltpu.VMEM((LANES, 2 * tq), F32)]
    return pl.pallas_call(
        kern,
        grid=(batch, A_HEADS // hps, nq),
        in_specs=[pl.BlockSpec((4, A_DK), lambda b, h, i: (0, 0)),
                  pl.BlockSpec((tq, wide), lambda b, h, i: (b * nq + i, BLK_AQ // hps + h)),
                  pl.BlockSpec((seq, wide), lambda b, h, i: (b, BLK_AK // hps + h)),
                  pl.BlockSpec((seq, wide), lambda b, h, i: (b, BLK_AV // hps + h)),
                  vec(), vec(), vec()],
        out_specs=pl.BlockSpec((tq, wide), lambda b, h, i: (b * nq + i, h)),
        out_shape=jax.ShapeDtypeStruct((batch * seq, BRANCH_WIDTH), F32),
        scratch_shapes=head_scratch * hps,
        compiler_params=pltpu.CompilerParams(
            dimension_semantics=("parallel", "parallel", "arbitrary"),
            vmem_limit_bytes=VMEM_LIMIT),
        name="diff_attn",
    )(lam_p, z, z, z, qg, kg, sg)


def _compress_kernel(z_ref, pos_ref, w1_ref, w2_ref, g_ref, kc_ref, vct_ref, zf_ref, x_ref,
                     *, seq):
    t_kv = pl.program_id(1)
    ncp = seq // CMP_STRIDE
    lo = lax.broadcasted_iota(jnp.int32, (1, LANES), 1) < 64
    zf_ref[0:seq, :] = z_ref[...].astype(F32)
    zf_ref[seq:, :] = jnp.zeros((CMP_STRIDE, LANES), F32)
    for t in range(CMP_LEN):
        rows = zf_ref[pl.ds(t, ncp, stride=CMP_STRIDE), :]
        x_ref[:, t * LANES:(t + 1) * LANES] = (rows + pos_ref[t:t + 1, :]).astype(BF16)
    h = jnp.dot(x_ref[...], w1_ref[...], preferred_element_type=F32)
    h = h * _sigmoid(h)
    y = jnp.dot(h.astype(BF16), w2_ref[...], preferred_element_type=F32)

    @pl.when(t_kv == 0)
    def _():
        kc_ref[...] = _half_norm(y, g_ref[...], lo).astype(BF16)

    @pl.when(t_kv != 0)
    def _():
        vct_ref[0:LANES, :] = y.T.astype(BF16)
        vct_ref[LANES:, :] = jnp.ones((ONES_ROWS, ncp), BF16)


def _compress(z, pos, w1, w2, gain, *, batch, seq):
    ncp = seq // CMP_STRIDE
    wide = CMP_LEN * LANES
    return pl.pallas_call(
        functools.partial(_compress_kernel, seq=seq),
        grid=(batch, 2),
        in_specs=[pl.BlockSpec((seq, LANES), lambda b, t: (b, BLK_BKV + t)),
                  pl.BlockSpec((None, CMP_LEN, LANES), lambda b, t: (t, 0, 0)),
                  pl.BlockSpec((None, wide, 2 * PHI_HIDDEN), lambda b, t: (t, 0, 0)),
                  pl.BlockSpec((None, 2 * PHI_HIDDEN, LANES), lambda b, t: (t, 0, 0)),
                  pl.BlockSpec((1, LANES), lambda b, t: (0, 0))],
        out_specs=[pl.BlockSpec((None, ncp, LANES), lambda b, t: (b, 0, 0)),
                   pl.BlockSpec((None, LANES + ONES_ROWS, ncp), lambda b, t: (b, 0, 0))],
        out_shape=[jax.ShapeDtypeStruct((batch, ncp, LANES), BF16),
                   jax.ShapeDtypeStruct((batch, LANES + ONES_ROWS, ncp), BF16)],
        scratch_shapes=[pltpu.VMEM((seq + CMP_STRIDE, LANES), F32),
                        pltpu.VMEM((ncp, wide), BF16)],
        compiler_params=pltpu.CompilerParams(
            dimension_semantics=("arbitrary", "arbitrary"), vmem_limit_bytes=VMEM_LIMIT),
        name="compress",
    )(z, pos, w1, w2, gain)


def _compress_weights(pos, w1, w2):
    pos2 = jnp.concatenate([pos, pos], axis=-1).astype(F32)
    w1r = w1.reshape(2, CMP_LEN, B_DK, PHI_HIDDEN)
    z1 = jnp.zeros_like(w1r)
    w1b = jnp.concatenate([jnp.concatenate([w1r, z1], axis=-1),
                           jnp.concatenate([z1, w1r], axis=-1)], axis=2)
    z2 = jnp.zeros_like(w2)
    w2b = jnp.concatenate([jnp.concatenate([w2, z2], axis=-1),
                           jnp.concatenate([z2, w2], axis=-1)], axis=1)
    return pos2, w1b.reshape(2, CMP_LEN * LANES, 2 * PHI_HIDDEN).astype(BF16), w2b.astype(BF16)


def _split2_bf16(x):
    hi = x.astype(BF16)
    return hi, (x - hi.astype(F32)).astype(BF16)


def _not_selected_t(val, k):
    nb, nqry = val.shape
    kf = float(k)
    sub = lax.broadcasted_iota(jnp.int32, (8, LANES), 0)

    def over_blocks(xs, op):
        xs = list(xs)
        while len(xs) > 1:
            xs = [op(xs[i], xs[i + 1]) for i in range(0, len(xs) - 1, 2)] + xs[len(xs) & ~1:]
        acc = xs[0]
        for sh in (4, 2, 1):
            acc = op(acc, pltpu.roll(acc, sh, 0))
        return acc

    out_cols = []
    for c in range(nqry // LANES):
        vals = [val[8 * r:8 * r + 8, c * LANES:(c + 1) * LANES] for r in range(nb // 8)]
        rem = vals
        thr = jnp.full((8, LANES), -jnp.inf, F32)
        done = jnp.zeros((8, LANES), F32)
        cnt = jnp.zeros((8, LANES), F32)
        for _ in range(k):
            top = over_blocks(rem, jnp.maximum)
            hit = [x == top for x in rem]
            cnt = cnt + over_blocks([jnp.where(h, 1.0, 0.0) for h in hit], jnp.add)
            rem = [jnp.where(h, -jnp.inf, x) for h, x in zip(hit, rem)]
            thr = jnp.where(done > 0.0, thr, top)
            done = jnp.where(cnt >= kf, 1.0, done)
        above = [jnp.where(v > thr, 1.0, 0.0) for v in vals]
        need = kf - over_blocks(above, jnp.add)
        offset = jnp.zeros((8, LANES), F32)
        cols = []
        for v, ab in zip(vals, above):
            tie = jnp.where(v == thr, 1.0, 0.0)
            scan = tie
            for sh in (1, 2, 4):
                scan = scan + jnp.where(sub >= sh, pltpu.roll(scan, sh, 0), 0.0)
            keep = ab + tie * jnp.where(offset + scan <= need, 1.0, 0.0)
            cols.append(1.0 - keep)
            offset = offset + over_blocks([tie], jnp.add)
        out_cols.append(jnp.concatenate(cols, axis=0))
    return jnp.concatenate(out_cols, axis=1)


def _nsa_kernel(q_ref, kc_ref, vct_ref, ks_ref, vs_ref, kw_ref, vw_ref, gt_ref, ovt_ref,
                qg_ref, ksg_ref, kwg_ref, o_ref,
                ksn_ref, kwn_ref, vst_ref, vwt_ref, sa_ref, sb_ref, m_ref, l_ref, acc_ref,
                part_ref, gsel_ref, *, tq, tk, seq):
    qi = pl.program_id(1)
    g = pl.program_id(2)
    ns = seq // SLC_LEN
    ncp = seq // CMP_STRIDE
    rows = B_HPG * tq
    half = LANES // 2
    lo = lax.broadcasted_iota(jnp.int32, (1, LANES), 1) < half
    g0 = (jnp.zeros((half, tq), jnp.int32) + g) == 0
    q0 = qi * tq

    @pl.when((qi == 0) & (g == 0))
    def _():
        _normalize_keys(kw_ref, kwn_ref, kwg_ref[...], lo, seq)
        _transpose_values(vs_ref, vst_ref, seq)
        _transpose_values(vw_ref, vwt_ref.at[0:LANES, :], seq)
        vwt_ref[LANES:, :] = jnp.ones((ONES_ROWS, seq), BF16)
        chunk = min(512, seq)

        def body(c, carry):
            r = pl.ds(pl.multiple_of(c * chunk, chunk), chunk)
            ksn_ref[r, 0:LANES] = _half_norm(ks_ref[r, :].astype(F32), ksg_ref[...], lo).astype(BF16)
            kblk = (c * chunk + lax.broadcasted_iota(jnp.int32, (chunk, LANES), 0)) // SLC_LEN
            own = lax.broadcasted_iota(jnp.int32, (chunk, LANES), 1) == kblk
            ksn_ref[r, LANES:2 * LANES] = jnp.where(own, NEG_INF, 0.0).astype(BF16)
            return carry
        lax.fori_loop(0, seq // chunk, body, 0)

    zeros = jnp.zeros((half, tq), F32)
    heads = []
    for pair in range(B_HPG // 2):
        xt = (_half_norm(q_ref[:, pair * LANES:(pair + 1) * LANES].astype(F32), qg_ref[...], lo)
              * QSCALE).T
        for hh in range(2):
            hrows = xt[hh * half:(hh + 1) * half]
            heads.append(jnp.concatenate([jnp.where(g0, hrows, zeros),
                                          jnp.where(g0, zeros, hrows)], axis=0))
    qs = jnp.concatenate(heads, axis=1).astype(BF16)
    qpos = q0 + (lax.broadcasted_iota(jnp.int32, (1, rows), 1) & (tq - 1))
    qpos1 = q0 + lax.broadcasted_iota(jnp.int32, (1, tq), 1)

    def per_head(bias):
        return jnp.concatenate([bias] * B_HPG, axis=1)

    gates = _sigmoid(gt_ref[...].astype(F32)).T
    g0row = (jnp.zeros((1, tq), jnp.int32) + g) == 0

    def gate_row(br):
        per_head = []
        for h in range(B_HPG):
            c = h * N_NSA_BRANCH + br
            c1 = B_HPG * N_NSA_BRANCH + c
            per_head.append(jnp.where(g0row, gates[c:c + 1], gates[c1:c1 + 1]))
        return jnp.concatenate(per_head, axis=1)

    s = jnp.dot(kc_ref[...], qs, preferred_element_type=F32)
    cmp_end = lax.broadcasted_iota(jnp.int32, (ncp, 1), 0) * CMP_STRIDE + (CMP_LEN - 1)
    s = s + per_head(jnp.where(cmp_end <= qpos1, 0.0, NEG_INF))
    e = jnp.exp2(s - jnp.max(s, axis=0, keepdims=True))
    res = jnp.dot(vct_ref[...], e.astype(BF16), preferred_element_type=F32)
    inv = jnp.where(qpos >= CMP_LEN - 1, 1.0 / res[LANES:LANES + 1], 0.0)
    p_cmp = e * inv
    o_cmp = res[0:LANES] * inv

    nk = WIN + tq
    start = pl.multiple_of(jnp.maximum(q0 - WIN, 0), tq)
    kpos = start + lax.broadcasted_iota(jnp.int32, (nk, 1), 0)
    sw = jnp.dot(kwn_ref[pl.ds(start, nk), :], qs, preferred_element_type=F32)
    sw = sw + per_head(jnp.where((kpos <= qpos1) & (kpos > qpos1 - WIN), 0.0, NEG_INF))
    pw = jnp.exp2(sw - jnp.max(sw, axis=0, keepdims=True))
    res = jnp.dot(vwt_ref[:, pl.ds(start, nk)], pw.astype(BF16), preferred_element_type=F32)
    o_win = res[0:LANES] / res[LANES:LANES + 1]

    part_ref[...] = gate_row(0) * o_cmp + gate_row(2) * o_win
    gsel_ref[...] = gate_row(1)

    psum = p_cmp[:, 0:tq]
    for h in range(1, B_HPG):
        psum = psum + p_cmp[:, h * tq:(h + 1) * tq]
    ovt = ovt_ref[...]
    imp_t = sum(jnp.dot(ovt, part, preferred_element_type=F32) for part in _split2_bf16(psum))
    blk = lax.broadcasted_iota(jnp.int32, (ns, tq), 0)
    cur = (q0 + lax.broadcasted_iota(jnp.int32, (ns, tq), 1)) // SLC_LEN
    val = jnp.where((blk == 0) | (blk == cur) | (blk == cur - 1), BIG, imp_t)
    val = jnp.where(blk > cur, NEG_INF, val)
    notsel = _not_selected_t(val, min(SLC_TOPK, ns))
    if ns < LANES:
        notsel = jnp.concatenate([notsel, jnp.zeros((LANES - ns, tq), F32)], axis=0)
    qa = jnp.concatenate([qs, jnp.concatenate([notsel.astype(BF16)] * B_HPG, axis=1)], axis=0)

    _reset(m_ref, l_ref, acc_ref)
    nfull = q0 // tk

    def rows_of(j):
        return pl.ds(pl.multiple_of(j * tk, tk), tk)

    def causal(sc):
        kpos = nfull * tk + lax.broadcasted_iota(jnp.int32, (tk, 1), 0)
        return sc + per_head(jnp.where(kpos <= qpos1, 0.0, NEG_INF))

    _causal_tiles_t(nfull, [(
        lambda j: jnp.dot(ksn_ref[rows_of(j), :], qa, preferred_element_type=F32),
        lambda j: vst_ref[:, rows_of(j)],
        causal, sa_ref, sb_ref, (m_ref, l_ref, acc_ref))])
    o = part_ref[...] + gsel_ref[...] * (acc_ref[...] / l_ref[...])
    out_t = jnp.concatenate(
        [jnp.where(g0, o[0:half, h * tq:(h + 1) * tq], o[half:LANES, h * tq:(h + 1) * tq])
         for h in range(B_HPG)], axis=0)
    o_ref[...] = out_t.T


def _nsa(z, kc, vc, ovt, qg, ksg, kwg, *, batch, seq):
    tq = min(256, seq)
    tk = min(512, seq)
    nq = seq // tq
    ns = seq // SLC_LEN
    ncp = seq // CMP_STRIDE
    rows = B_HPG * tq
    kern = functools.partial(_nsa_kernel, tq=tq, tk=tk, seq=seq)
    kv = lambda blk: pl.BlockSpec((seq, LANES), lambda b, i, g: (b, blk))
    vec = lambda: pl.BlockSpec((1, LANES), lambda b, i, g: (0, 0))
    assert seq >= WIN + tq and WIN % tq == 0 and tk % tq == 0
    return pl.pallas_call(
        kern,
        grid=(batch, nq, B_GROUPS),
        in_specs=[pl.BlockSpec((tq, 2 * LANES), lambda b, i, g: (b * nq + i, BLK_BQ // 2 + g)),
                  pl.BlockSpec((None, ncp, LANES), lambda b, i, g: (b, 0, 0)),
                  pl.BlockSpec((None, LANES + ONES_ROWS, ncp), lambda b, i, g: (b, 0, 0)),
                  kv(BLK_BKV + 2), kv(BLK_BKV + 3), kv(BLK_BKV + 4), kv(BLK_BKV + 5),
                  pl.BlockSpec((tq, LANES), lambda b, i, g: (b * nq + i, BLK_BBG)),
                  pl.BlockSpec((ns, ncp), lambda b, i, g: (0, 0)),
                  vec(), vec(), vec()],
        out_specs=pl.BlockSpec((tq, 2 * LANES), lambda b, i, g: (b * nq + i, g)),
        out_shape=jax.ShapeDtypeStruct((batch * seq, BRANCH_WIDTH), F32),
        scratch_shapes=[pltpu.VMEM((seq, 2 * LANES), BF16),
                        pltpu.VMEM((seq, LANES), BF16),
                        pltpu.VMEM((LANES, seq), BF16),
                        pltpu.VMEM((LANES + ONES_ROWS, seq), BF16),
                        pltpu.VMEM((tk, rows), F32),
                        pltpu.VMEM((tk, rows), F32),
                        pltpu.VMEM((1, rows), F32),
                        pltpu.VMEM((1, rows), F32),
                        pltpu.VMEM((LANES, rows), F32),
                        pltpu.VMEM((LANES, rows), F32),
                        pltpu.VMEM((1, rows), F32)],
        compiler_params=pltpu.CompilerParams(
            dimension_semantics=("parallel", "arbitrary", "arbitrary"),
            vmem_limit_bytes=VMEM_LIMIT),
        name="nsa",
    )(z, kc, vc, z, z, z, z, z, ovt, qg, ksg, kwg)


def _merge_kernel(x_ref, oa_ref, ob_ref, ag_ref, bg_ref, cu_ref, halo_ref, cg_ref,
                  mg0_ref, mg1_ref, mg2_ref, cw_ref, cs_ref, wb_ref, wo_ref, o_ref,
                  ext_ref, *, tm, seq):
    i = pl.program_id(0)
    tpos0 = (i * tm) % seq

    ext_ref[HALO:, :] = cu_ref[...].astype(F32)
    ext_ref[:HALO, :] = jnp.where(tpos0 == 0, 0.0, halo_ref[...].astype(F32))
    tpos = tpos0 + lax.broadcasted_iota(jnp.int32, (tm, 1), 0)
    pooled = []
    for gi, w in enumerate(C_WINDOWS):
        c = slice(gi * C_GDIM, (gi + 1) * C_GDIM)
        cur = ext_ref[HALO:, c]
        tot = cur
        for d in range(1, w):
            tot = tot + ext_ref[HALO - d:HALO - d + tm, c]
        cnt = jnp.minimum(tpos + 1, w).astype(F32)
        pg = (tot / cnt - cur).astype(BF16)
        pooled.append(jnp.dot(pg, cw_ref[gi], preferred_element_type=F32))
    oc = jnp.concatenate(pooled, axis=1) * cs_ref[...]

    def gated(o, gate_ref, k):
        gt = gate_ref[...].astype(F32)
        return jnp.dot((o * (gt * _sigmoid(gt))).astype(BF16), wb_ref[k],
                       preferred_element_type=F32)

    merged = (_sigmoid(mg0_ref[...].astype(F32)) * gated(oa_ref[...], ag_ref, 0)
              + _sigmoid(mg1_ref[...].astype(F32)) * gated(ob_ref[...], bg_ref, 1)
              + _sigmoid(mg2_ref[...].astype(F32)) * gated(oc, cg_ref, 2))
    o_ref[...] = x_ref[...] + jnp.dot(merged.astype(BF16), wo_ref[...], preferred_element_type=F32)


def _merge(x2d, z, oa, ob, cw, cs, wb, wo, *, seq):
    rows, d = x2d.shape
    tm = min(512, seq)
    bw = BRANCH_WIDTH
    kern = functools.partial(_merge_kernel, tm=tm, seq=seq)
    zcol = lambda blk: pl.BlockSpec((tm, bw), lambda i: (i, blk * LANES // bw))
    mgs = lambda k: pl.BlockSpec((tm, d), lambda i: (i, BLK_MG * LANES // d + k))
    full = lambda a: pl.BlockSpec(a.shape, lambda i: (0,) * a.ndim)
    return pl.pallas_call(
        kern,
        grid=(rows // tm,),
        in_specs=[pl.BlockSpec((tm, d), lambda i: (i, 0)),
                  pl.BlockSpec((tm, bw), lambda i: (i, 0)),
                  pl.BlockSpec((tm, bw), lambda i: (i, 0)),
                  zcol(BLK_AG), zcol(BLK_BG), zcol(BLK_CU),
                  pl.BlockSpec((HALO, bw),
                               lambda i: (jnp.maximum(i * (tm // HALO) - 1, 0), BLK_CU * LANES // bw)),
                  zcol(BLK_CG), mgs(0), mgs(1), mgs(2),
                  full(cw), full(cs), full(wb), full(wo)],
        out_specs=pl.BlockSpec((tm, d), lambda i: (i, 0)),
        out_shape=jax.ShapeDtypeStruct((rows, d), F32),
        scratch_shapes=[pltpu.VMEM((tm + HALO, bw), F32)],
        compiler_params=pltpu.CompilerParams(
            dimension_semantics=("parallel",), vmem_limit_bytes=VMEM_LIMIT),
        name="merge",
    )(x2d, oa, ob, z, z, z, z, z, z, z, z, cw, cs, wb, wo)


def _overlap_t(seq):
    ncp = seq // CMP_STRIDE
    nc = (seq - CMP_LEN) // CMP_STRIDE + 1
    ns = seq // SLC_LEN
    cs = np.arange(ncp) * CMP_STRIDE
    ce = cs + CMP_LEN
    ss = np.arange(ns) * SLC_LEN
    se = ss + SLC_LEN
    ov = np.clip(np.minimum(ce[None, :], se[:, None]) - np.maximum(cs[None, :], ss[:, None]), 0, None)
    ov = ov / CMP_LEN
    ov[:, nc:] = 0.0
    return jnp.asarray(ov, dtype=BF16)


def kernel(x, norm_g, w_in, a_q_g, a_k_g, a_lam, a_subln_g, b_q_g, b_k_g, b_cmp_pos, b_phi_w1,
           b_phi_w2, c_w, c_scale, w_branch, w_out):
    batch, seq, d = x.shape
    x2d = x.reshape(batch * seq, d)
    ovt = _overlap_t(seq)
    two = lambda v: jnp.concatenate([v, v]).reshape(1, LANES).astype(F32)
    for l in range(DEPTH):
        z = _inproj(x2d, norm_g[l].reshape(1, d), _permute_w_in(w_in[l].astype(BF16)))
        lam_init = 0.8 - 0.6 * math.exp(-0.3 * l)
        oa = _diff_attn(z, a_lam[l].astype(F32), two(a_q_g[l]), two(a_k_g[l]),
                        a_subln_g[l].reshape(1, LANES).astype(F32),
                        batch=batch, seq=seq, lam_init=lam_init)
        kc, vc = _compress(z, *_compress_weights(b_cmp_pos[l], b_phi_w1[l], b_phi_w2[l]),
                           two(b_k_g[l, 0]), batch=batch, seq=seq)
        ob = _nsa(z, kc, vc, ovt, two(b_q_g[l]), two(b_k_g[l, 1]), two(b_k_g[l, 2]),
                  batch=batch, seq=seq)
        x2d = _merge(x2d, z, oa, ob, c_w[l].astype(BF16), c_scale[l].reshape(1, BRANCH_WIDTH),
                     w_branch[l].astype(BF16), w_out[l].astype(BF16), seq=seq)
    return x2d.reshape(batch, seq, d)
```

```python
import functools
import math

import numpy as np
import jax
import jax.numpy as jnp
from jax import lax
from jax.experimental import pallas as pl
from jax.experimental.pallas import tpu as pltpu

F32 = jnp.float32
BF16 = jnp.bfloat16

D_MODEL = 1024
DEPTH = 2
BRANCH_WIDTH = D_MODEL // 2
A_DK = 64
A_DV = 2 * A_DK
A_HEADS = BRANCH_WIDTH // A_DV
B_DK = 64
B_HEADS = BRANCH_WIDTH // B_DK
B_GROUPS = 2
B_HPG = B_HEADS // B_GROUPS
CMP_LEN = 32
CMP_STRIDE = 16
SLC_LEN = 64
SLC_TOPK = 16
WIN = 512
PHI_HIDDEN = 256
N_NSA_BRANCH = 3
C_WINDOWS = (2, 4, 8, 16)
C_GDIM = BRANCH_WIDTH // len(C_WINDOWS)
N_BRANCH = 3
EPS = 1e-6
NEG_INF = -1e30
BIG = 1e30

LANES = 128
HALO = 16
ONES_ROWS = 16
VMEM_LIMIT = 48 * 1024 * 1024

ZW = 8192
BLK_AQ, BLK_AK, BLK_AV, BLK_AG = 0, 4, 8, 12
BLK_BQ, BLK_BG, BLK_CU, BLK_CG = 16, 20, 24, 28
BLK_MG = 32
BLK_BKV = 56
BLK_BBG = 62

SCALE = 0.125
QSCALE = SCALE * math.log2(math.e)


def _permute_w_in(w):
    o = np.cumsum([0, 512, 512, 512, 512, 512, 768, 24, 512, 512, 512, 3072])
    seg = lambda i: w[:, int(o[i]):int(o[i + 1])]
    aq, ak, av, ag, bq, bkv, bbg, bg, cu, cg, mg = [seg(i) for i in range(11)]
    parts = [aq, ak, av, ag * 0.5, bq, bg * 0.5, cu, cg * 0.5, mg * 0.5, bkv, bbg]
    used = sum(p.shape[1] for p in parts)
    parts.append(jnp.zeros((w.shape[0], ZW - used), w.dtype))
    return jnp.concatenate(parts, axis=1)


def _sigmoid(x):
    return 0.5 * jnp.tanh(0.5 * x) + 0.5


def _half_norm(x, gain, lo):
    x2 = x * x
    s_lo = jnp.sum(jnp.where(lo, x2, 0.0), axis=-1, keepdims=True)
    s_hi = jnp.sum(jnp.where(lo, 0.0, x2), axis=-1, keepdims=True)
    ms = jnp.where(lo, s_lo, s_hi) * (1.0 / 64.0)
    return x * lax.rsqrt(ms + EPS) * gain


def _inproj_kernel(x_ref, g_ref, w_ref, o_ref, h_ref):
    @pl.when(pl.program_id(1) == 0)
    def _():
        x = x_ref[...]
        ms = jnp.mean(x * x, axis=-1, keepdims=True)
        h_ref[...] = (x * lax.rsqrt(ms + EPS) * g_ref[...]).astype(BF16)

    o_ref[...] = jnp.dot(h_ref[...], w_ref[...], preferred_element_type=F32).astype(o_ref.dtype)


def _inproj(x2d, g, w):
    rows, d = x2d.shape
    tm, tn = min(1024, rows), 2048
    return pl.pallas_call(
        _inproj_kernel,
        grid=(rows // tm, ZW // tn),
        in_specs=[pl.BlockSpec((tm, d), lambda i, j: (i, 0)),
                  pl.BlockSpec((1, d), lambda i, j: (0, 0)),
                  pl.BlockSpec((d, tn), lambda i, j: (0, j))],
        out_specs=pl.BlockSpec((tm, tn), lambda i, j: (i, j)),
        out_shape=jax.ShapeDtypeStruct((rows, ZW), BF16),
        scratch_shapes=[pltpu.VMEM((tm, d), BF16)],
        compiler_params=pltpu.CompilerParams(
            dimension_semantics=("parallel", "arbitrary"), vmem_limit_bytes=VMEM_LIMIT),
        name="inproj",
    )(x2d, g, w)


def _reset(m_ref, l_ref, acc_ref):
    m_ref[...] = jnp.full(m_ref.shape, NEG_INF, F32)
    l_ref[...] = jnp.zeros(l_ref.shape, F32)
    acc_ref[...] = jnp.zeros(acc_ref.shape, F32)


def _softmax_step_t(s, vt_tile, m_ref, l_ref, acc_ref):
    m_old = m_ref[...]
    m_new = jnp.maximum(m_old, jnp.max(s, axis=0, keepdims=True))
    alpha = jnp.exp2(m_old - m_new)
    p = jnp.exp2(s - m_new)
    l_ref[...] = alpha * l_ref[...] + jnp.sum(p, axis=0, keepdims=True)
    acc_ref[...] = alpha * acc_ref[...] + jnp.dot(vt_tile, p.astype(BF16),
                                                  preferred_element_type=F32)
    m_ref[...] = m_new


def _causal_tiles_t(nfull, chains):
    def issue(j, into_b):
        for scores_fn, _, _, sa_ref, sb_ref, _ in chains:
            (sb_ref if into_b else sa_ref)[...] = scores_fn(j)

    def consume(j, from_b, masked=False):
        for _, value_fn, mask_fn, sa_ref, sb_ref, state in chains:
            s = (sb_ref if from_b else sa_ref)[...]
            _softmax_step_t(mask_fn(s) if masked else s, value_fn(j), *state)

    def two_tiles(j):
        issue(j + 1, True)
        consume(j, False)
        issue(j + 2, False)
        consume(j + 1, True)

    issue(0, False)
    done = 0
    for per_trip in (8, 4, 2):
        def trip(i, carry, per_trip=per_trip, done=done):
            for t in range(0, per_trip, 2):
                two_tiles(done + per_trip * i + t)
            return carry
        ntrip = (nfull - done) // per_trip
        lax.fori_loop(0, ntrip, trip, 0)
        done = done + per_trip * ntrip
    odd = (nfull & 1) == 1

    @pl.when(odd)
    def _():
        issue(nfull, True)
        consume(nfull - 1, False)
        consume(nfull, True, masked=True)

    @pl.when(jnp.logical_not(odd))
    def _():
        consume(nfull, False, masked=True)


def _transpose_values(v_ref, vt_ref, seq, chunk=512):
    chunk = min(chunk, seq)

    def body(c, carry):
        r = pl.ds(pl.multiple_of(c * chunk, chunk), chunk)
        vt_ref[:, r] = v_ref[r, :].astype(F32).T.astype(BF16)
        return carry
    lax.fori_loop(0, seq // chunk, body, 0)


def _normalize_keys(src_ref, dst_ref, gain, lo, seq, chunk=512):
    def body(c, carry):
        r = pl.ds(pl.multiple_of(c * chunk, chunk), chunk)
        dst_ref[r, :] = _half_norm(src_ref[r, :].astype(F32), gain, lo).astype(BF16)
        return carry
    lax.fori_loop(0, seq // chunk, body, 0)


A_HEADS_PER_STEP = 1


def _diff_attn_kernel(lam_ref, q_ref, k_ref, v_ref, qg_ref, kg_ref, sg_ref, o_ref, *scratch,
                      tq, tk, seq, lam_init):
    qi = pl.program_id(2)
    lo = lax.broadcasted_iota(jnp.int32, (1, LANES), 1) < 64
    per_head = [scratch[7 * h:7 * h + 7] for h in range(A_HEADS_PER_STEP)]

    @pl.when(qi == 0)
    def _():
        for h, (kn_ref, vt_ref, *_) in enumerate(per_head):
            lanes = slice(h * LANES, (h + 1) * LANES)
            _normalize_keys(k_ref.at[:, lanes], kn_ref, kg_ref[...], lo, seq)
            _transpose_values(v_ref.at[:, lanes], vt_ref, seq)

    q0 = qi * tq
    nfull = q0 // tk
    top = lax.broadcasted_iota(jnp.int32, (LANES, 1), 0) < 64

    def rows_of(j):
        return pl.ds(pl.multiple_of(j * tk, tk), tk)

    def causal(s):
        qpos = q0 + lax.broadcasted_iota(jnp.int32, (tk, tq), 1)
        kpos = nfull * tk + lax.broadcasted_iota(jnp.int32, (tk, tq), 0)
        bias = jnp.where(kpos <= qpos, 0.0, NEG_INF)
        return s + jnp.concatenate([bias, bias], axis=1)

    def chain(h):
        kn_ref, vt_ref, sa_ref, sb_ref, m_ref, l_ref, acc_ref = per_head[h]
        q = q_ref[:, h * LANES:(h + 1) * LANES].astype(F32)
        qt = (_half_norm(q, qg_ref[...], lo) * QSCALE).T
        qq = jnp.concatenate([jnp.where(top, qt, 0.0), jnp.where(top, 0.0, qt)],
                             axis=1).astype(BF16)
        _reset(m_ref, l_ref, acc_ref)
        return (lambda j: jnp.dot(kn_ref[rows_of(j), :], qq, preferred_element_type=F32),
                lambda j: vt_ref[:, rows_of(j)],
                causal, sa_ref, sb_ref, (m_ref, l_ref, acc_ref))

    _causal_tiles_t(nfull, [chain(h) for h in range(A_HEADS_PER_STEP)])

    lp = lam_ref[...]
    lam = (jnp.exp(jnp.sum(lp[0:1] * lp[1:2], axis=-1, keepdims=True))
           - jnp.exp(jnp.sum(lp[2:3] * lp[3:4], axis=-1, keepdims=True)) + lam_init)
    for h, (_, _, _, _, _, l_ref, acc_ref) in enumerate(per_head):
        acc = acc_ref[...]
        l = l_ref[...]
        o = acc[:, :tq] / l[:, :tq] - lam * (acc[:, tq:] / l[:, tq:])
        ms = jnp.mean(o * o, axis=0, keepdims=True)
        o_ref[:, h * LANES:(h + 1) * LANES] = ((o * lax.rsqrt(ms + EPS)).T * sg_ref[...]
                                               * (1.0 - lam_init))


def _diff_attn(z, lam_p, qg, kg, sg, *, batch, seq, lam_init):
    tq = min(512, seq)
    tk = min(512, seq)
    nq = seq // tq
    hps = A_HEADS_PER_STEP
    wide = hps * LANES
    kern = functools.partial(_diff_attn_kernel, tq=tq, tk=tk, seq=seq, lam_init=lam_init)
    vec = lambda: pl.BlockSpec((1, LANES), lambda b, h, i: (0, 0))
    head_scratch = [pltpu.VMEM((seq, LANES), BF16),
                    pltpu.VMEM((LANES, seq), BF16),
                    pltpu.VMEM((tk, 2 * tq), F32),
                    pltpu.VMEM((tk, 2 * tq), F32),
                    pltpu.VMEM((1, 2 * tq), F32),
                    pltpu.VMEM((1, 2 * tq), F32),
                    pltpu.VMEM((LANES, 2 * tq), F32)]
    return pl.pallas_call(
        kern,
        grid=(batch, A_HEADS // hps, nq),
        in_specs=[pl.BlockSpec((4, A_DK), lambda b, h, i: (0, 0)),
                  pl.BlockSpec((tq, wide), lambda b, h, i: (b * nq + i, BLK_AQ // hps + h)),
                  pl.BlockSpec((seq, wide), lambda b, h, i: (b, BLK_AK // hps + h)),
                  pl.BlockSpec((seq, wide), lambda b, h, i: (b, BLK_AV // hps + h)),
                  vec(), vec(), vec()],
        out_specs=pl.BlockSpec((tq, wide), lambda b, h, i: (b * nq + i, h)),
        out_shape=jax.ShapeDtypeStruct((batch * seq, BRANCH_WIDTH), F32),
        scratch_shapes=head_scratch * hps,
        compiler_params=pltpu.CompilerParams(
            dimension_semantics=("parallel", "parallel", "arbitrary"),
            vmem_limit_bytes=VMEM_LIMIT),
        name="diff_attn",
    )(lam_p, z, z, z, qg, kg, sg)


def _compress_kernel(z_ref, pos_ref, w1_ref, w2_ref, g_ref, kc_ref, vct_ref, zf_ref, x_ref,
                     *, seq):
    t_kv = pl.program_id(1)
    ncp = seq // CMP_STRIDE
    lo = lax.broadcasted_iota(jnp.int32, (1, LANES), 1) < 64
    zf_ref[0:seq, :] = z_ref[...].astype(F32)
    zf_ref[seq:, :] = jnp.zeros((CMP_STRIDE, LANES), F32)
    for t in range(CMP_LEN):
        rows = zf_ref[pl.ds(t, ncp, stride=CMP_STRIDE), :]
        x_ref[:, t * LANES:(t + 1) * LANES] = (rows + pos_ref[t:t + 1, :]).astype(BF16)
    h = jnp.dot(x_ref[...], w1_ref[...], preferred_element_type=F32)
    h = h * _sigmoid(h)
    y = jnp.dot(h.astype(BF16), w2_ref[...], preferred_element_type=F32)

    @pl.when(t_kv == 0)
    def _():
        kc_ref[...] = _half_norm(y, g_ref[...], lo).astype(BF16)

    @pl.when(t_kv != 0)
    def _():
        vct_ref[0:LANES, :] = y.T.astype(BF16)
        vct_ref[LANES:, :] = jnp.ones((ONES_ROWS, ncp), BF16)


def _compress(z, pos, w1, w2, gain, *, batch, seq):
    ncp = seq // CMP_STRIDE
    wide = CMP_LEN * LANES
    return pl.pallas_call(
        functools.partial(_compress_kernel, seq=seq),
        grid=(batch, 2),
        in_specs=[pl.BlockSpec((seq, LANES), lambda b, t: (b, BLK_BKV + t)),
                  pl.BlockSpec((None, CMP_LEN, LANES), lambda b, t: (t, 0, 0)),
                  pl.BlockSpec((None, wide, 2 * PHI_HIDDEN), lambda b, t: (t, 0, 0)),
                  pl.BlockSpec((None, 2 * PHI_HIDDEN, LANES), lambda b, t: (t, 0, 0)),
                  pl.BlockSpec((1, LANES), lambda b, t: (0, 0))],
        out_specs=[pl.BlockSpec((None, ncp, LANES), lambda b, t: (b, 0, 0)),
                   pl.BlockSpec((None, LANES + ONES_ROWS, ncp), lambda b, t: (b, 0, 0))],
        out_shape=[jax.ShapeDtypeStruct((batch, ncp, LANES), BF16),
                   jax.ShapeDtypeStruct((batch, LANES + ONES_ROWS, ncp), BF16)],
        scratch_shapes=[pltpu.VMEM((seq + CMP_STRIDE, LANES), F32),
                        pltpu.VMEM((ncp, wide), BF16)],
        compiler_params=pltpu.CompilerParams(
            dimension_semantics=("arbitrary", "arbitrary"), vmem_limit_bytes=VMEM_LIMIT),
        name="compress",
    )(z, pos, w1, w2, gain)


def _compress_weights(pos, w1, w2):
    pos2 = jnp.concatenate([pos, pos], axis=-1).astype(F32)
    w1r = w1.reshape(2, CMP_LEN, B_DK, PHI_HIDDEN)
    z1 = jnp.zeros_like(w1r)
    w1b = jnp.concatenate([jnp.concatenate([w1r, z1], axis=-1),
                           jnp.concatenate([z1, w1r], axis=-1)], axis=2)
    z2 = jnp.zeros_like(w2)
    w2b = jnp.concatenate([jnp.concatenate([w2, z2], axis=-1),
                           jnp.concatenate([z2, w2], axis=-1)], axis=1)
    return pos2, w1b.reshape(2, CMP_LEN * LANES, 2 * PHI_HIDDEN).astype(BF16), w2b.astype(BF16)


def _split2_bf16(x):
    hi = x.astype(BF16)
    return hi, (x - hi.astype(F32)).astype(BF16)


def _not_selected_t(val, k):
    nb, nqry = val.shape
    kf = float(k)
    sub = lax.broadcasted_iota(jnp.int32, (8, nqry), 0)

    def over_blocks(xs, op):
        xs = list(xs)
        while len(xs) > 1:
            xs = [op(xs[i], xs[i + 1]) for i in range(0, len(xs) - 1, 2)] + xs[len(xs) & ~1:]
        acc = xs[0]
        for sh in (4, 2, 1):
            acc = op(acc, pltpu.roll(acc, sh, 0))
        return acc

    vals = [val[8 * r:8 * r + 8] for r in range(nb // 8)]
    rem = vals
    thr = jnp.full((8, nqry), -jnp.inf, F32)
    done = jnp.zeros((8, nqry), F32)
    cnt = jnp.zeros((8, nqry), F32)
    for _ in range(k):
        top = over_blocks(rem, jnp.maximum)
        hit = [x == top for x in rem]
        cnt = cnt + over_blocks([jnp.where(h, 1.0, 0.0) for h in hit], jnp.add)
        rem = [jnp.where(h, -jnp.inf, x) for h, x in zip(hit, rem)]
        thr = jnp.where(done > 0.0, thr, top)
        done = jnp.where(cnt >= kf, 1.0, done)
    above = [jnp.where(v > thr, 1.0, 0.0) for v in vals]
    need = kf - over_blocks(above, jnp.add)
    offset = jnp.zeros((8, nqry), F32)
    out = []
    for v, ab in zip(vals, above):
        tie = jnp.where(v == thr, 1.0, 0.0)
        scan = tie
        for sh in (1, 2, 4):
            scan = scan + jnp.where(sub >= sh, pltpu.roll(scan, sh, 0), 0.0)
        keep = ab + tie * jnp.where(offset + scan <= need, 1.0, 0.0)
        out.append(1.0 - keep)
        offset = offset + over_blocks([tie], jnp.add)
    return jnp.concatenate(out, axis=0)


def _nsa_kernel(q_ref, kc_ref, vct_ref, ks_ref, vs_ref, kw_ref, vw_ref, gt_ref, ovt_ref,
                qg_ref, ksg_ref, kwg_ref, o_ref,
                ksn_ref, kwn_ref, vst_ref, vwt_ref, sa_ref, sb_ref, m_ref, l_ref, acc_ref,
                part_ref, gsel_ref, *, tq, tk, seq):
    qi = pl.program_id(1)
    g = pl.program_id(2)
    ns = seq // SLC_LEN
    ncp = seq // CMP_STRIDE
    rows = B_HPG * tq
    half = LANES // 2
    lo = lax.broadcasted_iota(jnp.int32, (1, LANES), 1) < half
    g0 = (jnp.zeros((half, tq), jnp.int32) + g) == 0
    q0 = qi * tq

    @pl.when((qi == 0) & (g == 0))
    def _():
        _normalize_keys(kw_ref, kwn_ref, kwg_ref[...], lo, seq)
        _transpose_values(vs_ref, vst_ref, seq)
        _transpose_values(vw_ref, vwt_ref.at[0:LANES, :], seq)
        vwt_ref[LANES:, :] = jnp.ones((ONES_ROWS, seq), BF16)
        chunk = min(512, seq)

        def body(c, carry):
            r = pl.ds(pl.multiple_of(c * chunk, chunk), chunk)
            ksn_ref[r, 0:LANES] = _half_norm(ks_ref[r, :].astype(F32), ksg_ref[...], lo).astype(BF16)
            kblk = (c * chunk + lax.broadcasted_iota(jnp.int32, (chunk, LANES), 0)) // SLC_LEN
            own = lax.broadcasted_iota(jnp.int32, (chunk, LANES), 1) == kblk
            ksn_ref[r, LANES:2 * LANES] = jnp.where(own, NEG_INF, 0.0).astype(BF16)
            return carry
        lax.fori_loop(0, seq // chunk, body, 0)

    zeros = jnp.zeros((half, tq), F32)
    heads = []
    for pair in range(B_HPG // 2):
        xt = (_half_norm(q_ref[:, pair * LANES:(pair + 1) * LANES].astype(F32), qg_ref[...], lo)
              * QSCALE).T
        for hh in range(2):
            hrows = xt[hh * half:(hh + 1) * half]
            heads.append(jnp.concatenate([jnp.where(g0, hrows, zeros),
                                          jnp.where(g0, zeros, hrows)], axis=0))
    qs = jnp.concatenate(heads, axis=1).astype(BF16)
    qpos = q0 + (lax.broadcasted_iota(jnp.int32, (1, rows), 1) & (tq - 1))
    qpos1 = q0 + lax.broadcasted_iota(jnp.int32, (1, tq), 1)

    def per_head(bias):
        return jnp.concatenate([bias] * B_HPG, axis=1)

    gates = _sigmoid(gt_ref[...].astype(F32)).T
    g0row = (jnp.zeros((1, tq), jnp.int32) + g) == 0

    def gate_row(br):
        per_head = []
        for h in range(B_HPG):
            c = h * N_NSA_BRANCH + br
            c1 = B_HPG * N_NSA_BRANCH + c
            per_head.append(jnp.where(g0row, gates[c:c + 1], gates[c1:c1 + 1]))
        return jnp.concatenate(per_head, axis=1)

    s = jnp.dot(kc_ref[...], qs, preferred_element_type=F32)
    cmp_end = lax.broadcasted_iota(jnp.int32, (ncp, 1), 0) * CMP_STRIDE + (CMP_LEN - 1)
    s = s + per_head(jnp.where(cmp_end <= qpos1, 0.0, NEG_INF))
    e = jnp.exp2(s - jnp.max(s, axis=0, keepdims=True))
    res = jnp.dot(vct_ref[...], e.astype(BF16), preferred_element_type=F32)
    inv = jnp.where(qpos >= CMP_LEN - 1, 1.0 / res[LANES:LANES + 1], 0.0)
    p_cmp = e * inv
    o_cmp = res[0:LANES] * inv

    nk = WIN + tq
    start = pl.multiple_of(jnp.maximum(q0 - WIN, 0), tq)
    kpos = start + lax.broadcasted_iota(jnp.int32, (nk, 1), 0)
    sw = jnp.dot(kwn_ref[pl.ds(start, nk), :], qs, preferred_element_type=F32)
    sw = sw + per_head(jnp.where((kpos <= qpos1) & (kpos > qpos1 - WIN), 0.0, NEG_INF))
    pw = jnp.exp2(sw - jnp.max(sw, axis=0, keepdims=True))
    res = jnp.dot(vwt_ref[:, pl.ds(start, nk)], pw.astype(BF16), preferred_element_type=F32)
    o_win = res[0:LANES] / res[LANES:LANES + 1]

    part_ref[...] = gate_row(0) * o_cmp + gate_row(2) * o_win
    gsel_ref[...] = gate_row(1)

    psum = p_cmp[:, 0:tq]
    for h in range(1, B_HPG):
        psum = psum + p_cmp[:, h * tq:(h + 1) * tq]
    ovt = ovt_ref[...]
    imp_t = sum(jnp.dot(ovt, part, preferred_element_type=F32) for part in _split2_bf16(psum))
    blk = lax.broadcasted_iota(jnp.int32, (ns, tq), 0)
    cur = (q0 + lax.broadcasted_iota(jnp.int32, (ns, tq), 1)) // SLC_LEN
    val = jnp.where((blk == 0) | (blk == cur) | (blk == cur - 1), BIG, imp_t)
    val = jnp.where(blk > cur, NEG_INF, val)
    notsel = _not_selected_t(val, min(SLC_TOPK, ns))
    if ns < LANES:
        notsel = jnp.concatenate([notsel, jnp.zeros((LANES - ns, tq), F32)], axis=0)
    qa = jnp.concatenate([qs, jnp.concatenate([notsel.astype(BF16)] * B_HPG, axis=1)], axis=0)

    _reset(m_ref, l_ref, acc_ref)
    nfull = q0 // tk

    def rows_of(j):
        return pl.ds(pl.multiple_of(j * tk, tk), tk)

    def causal(sc):
        kpos = nfull * tk + lax.broadcasted_iota(jnp.int32, (tk, 1), 0)
        return sc + per_head(jnp.where(kpos <= qpos1, 0.0, NEG_INF))

    _causal_tiles_t(nfull, [(
        lambda j: jnp.dot(ksn_ref[rows_of(j), :], qa, preferred_element_type=F32),
        lambda j: vst_ref[:, rows_of(j)],
        causal, sa_ref, sb_ref, (m_ref, l_ref, acc_ref))])
    o = part_ref[...] + gsel_ref[...] * (acc_ref[...] / l_ref[...])
    out_t = jnp.concatenate(
        [jnp.where(g0, o[0:half, h * tq:(h + 1) * tq], o[half:LANES, h * tq:(h + 1) * tq])
         for h in range(B_HPG)], axis=0)
    o_ref[...] = out_t.T


def _nsa(z, kc, vc, ovt, qg, ksg, kwg, *, batch, seq):
    tq = min(256, seq)
    tk = min(512, seq)
    nq = seq // tq
    ns = seq // SLC_LEN
    ncp = seq // CMP_STRIDE
    rows = B_HPG * tq
    kern = functools.partial(_nsa_kernel, tq=tq, tk=tk, seq=seq)
    kv = lambda blk: pl.BlockSpec((seq, LANES), lambda b, i, g: (b, blk))
    vec = lambda: pl.BlockSpec((1, LANES), lambda b, i, g: (0, 0))
    assert seq >= WIN + tq and WIN % tq == 0 and tk % tq == 0
    return pl.pallas_call(
        kern,
        grid=(batch, nq, B_GROUPS),
        in_specs=[pl.BlockSpec((tq, 2 * LANES), lambda b, i, g: (b * nq + i, BLK_BQ // 2 + g)),
                  pl.BlockSpec((None, ncp, LANES), lambda b, i, g: (b, 0, 0)),
                  pl.BlockSpec((None, LANES + ONES_ROWS, ncp), lambda b, i, g: (b, 0, 0)),
                  kv(BLK_BKV + 2), kv(BLK_BKV + 3), kv(BLK_BKV + 4), kv(BLK_BKV + 5),
                  pl.BlockSpec((tq, LANES), lambda b, i, g: (b * nq + i, BLK_BBG)),
                  pl.BlockSpec((ns, ncp), lambda b, i, g: (0, 0)),
                  vec(), vec(), vec()],
        out_specs=pl.BlockSpec((tq, 2 * LANES), lambda b, i, g: (b * nq + i, g)),
        out_shape=jax.ShapeDtypeStruct((batch * seq, BRANCH_WIDTH), F32),
        scratch_shapes=[pltpu.VMEM((seq, 2 * LANES), BF16),
                        pltpu.VMEM((seq, LANES), BF16),
                        pltpu.VMEM((LANES, seq), BF16),
                        pltpu.VMEM((LANES + ONES_ROWS, seq), BF16),
                        pltpu.VMEM((tk, rows), F32),
                        pltpu.VMEM((tk, rows), F32),
                        pltpu.VMEM((1, rows), F32),
                        pltpu.VMEM((1, rows), F32),
                        pltpu.VMEM((LANES, rows), F32),
                        pltpu.VMEM((LANES, rows), F32),
                        pltpu.VMEM((1, rows), F32)],
        compiler_params=pltpu.CompilerParams(
            dimension_semantics=("parallel", "arbitrary", "arbitrary"),
            vmem_limit_bytes=VMEM_LIMIT),
        name="nsa",
    )(z, kc, vc, z, z, z, z, z, ovt, qg, ksg, kwg)


def _merge_kernel(x_ref, oa_ref, ob_ref, ag_ref, bg_ref, cu_ref, halo_ref, cg_ref,
                  mg0_ref, mg1_ref, mg2_ref, cw_ref, cs_ref, wb_ref, wo_ref, o_ref,
                  ext_ref, *, tm, seq):
    i = pl.program_id(0)
    tpos0 = (i * tm) % seq

    ext_ref[HALO:, :] = cu_ref[...].astype(F32)
    ext_ref[:HALO, :] = jnp.where(tpos0 == 0, 0.0, halo_ref[...].astype(F32))
    tpos = tpos0 + lax.broadcasted_iota(jnp.int32, (tm, 1), 0)
    pooled = []
    for gi, w in enumerate(C_WINDOWS):
        c = slice(gi * C_GDIM, (gi + 1) * C_GDIM)
        cur = ext_ref[HALO:, c]
        tot = cur
        for d in range(1, w):
            tot = tot + ext_ref[HALO - d:HALO - d + tm, c]
        cnt = jnp.minimum(tpos + 1, w).astype(F32)
        pg = (tot / cnt - cur).astype(BF16)
        pooled.append(jnp.dot(pg, cw_ref[gi], preferred_element_type=F32))
    oc = jnp.concatenate(pooled, axis=1) * cs_ref[...]

    def gated(o, gate_ref, k):
        u = gate_ref[...].astype(F32)
        return jnp.dot((o * (u + u * jnp.tanh(u))).astype(BF16), wb_ref[k],
                       preferred_element_type=F32)

    def merge_gate(mg_ref, y):
        return y + jnp.tanh(mg_ref[...].astype(F32)) * y

    merged = (merge_gate(mg0_ref, gated(oa_ref[...], ag_ref, 0))
              + merge_gate(mg1_ref, gated(ob_ref[...], bg_ref, 1))
              + merge_gate(mg2_ref, gated(oc, cg_ref, 2)))
    o_ref[...] = x_ref[...] + jnp.dot(merged.astype(BF16), wo_ref[...], preferred_element_type=F32)


def _merge(x2d, z, oa, ob, cw, cs, wb, wo, *, seq):
    rows, d = x2d.shape
    tm = min(512, seq)
    bw = BRANCH_WIDTH
    kern = functools.partial(_merge_kernel, tm=tm, seq=seq)
    zcol = lambda blk: pl.BlockSpec((tm, bw), lambda i: (i, blk * LANES // bw))
    mgs = lambda k: pl.BlockSpec((tm, d), lambda i: (i, BLK_MG * LANES // d + k))
    full = lambda a: pl.BlockSpec(a.shape, lambda i: (0,) * a.ndim)
    return pl.pallas_call(
        kern,
        grid=(rows // tm,),
        in_specs=[pl.BlockSpec((tm, d), lambda i: (i, 0)),
                  pl.BlockSpec((tm, bw), lambda i: (i, 0)),
                  pl.BlockSpec((tm, bw), lambda i: (i, 0)),
                  zcol(BLK_AG), zcol(BLK_BG), zcol(BLK_CU),
                  pl.BlockSpec((HALO, bw),
                               lambda i: (jnp.maximum(i * (tm // HALO) - 1, 0), BLK_CU * LANES // bw)),
                  zcol(BLK_CG), mgs(0), mgs(1), mgs(2),
                  full(cw), full(cs), full(wb), full(wo)],
        out_specs=pl.BlockSpec((tm, d), lambda i: (i, 0)),
        out_shape=jax.ShapeDtypeStruct((rows, d), F32),
        scratch_shapes=[pltpu.VMEM((tm + HALO, bw), F32)],
        compiler_params=pltpu.CompilerParams(
            dimension_semantics=("parallel",), vmem_limit_bytes=VMEM_LIMIT),
        name="merge",
    )(x2d, oa, ob, z, z, z, z, z, z, z, z, cw, cs, wb, wo)


def _overlap_t(seq):
    ncp = seq // CMP_STRIDE
    nc = (seq - CMP_LEN) // CMP_STRIDE + 1
    ns = seq // SLC_LEN
    cs = np.arange(ncp) * CMP_STRIDE
    ce = cs + CMP_LEN
    ss = np.arange(ns) * SLC_LEN
    se = ss + SLC_LEN
    ov = np.clip(np.minimum(ce[None, :], se[:, None]) - np.maximum(cs[None, :], ss[:, None]), 0, None)
    ov = ov / CMP_LEN
    ov[:, nc:] = 0.0
    return jnp.asarray(ov, dtype=BF16)


def kernel(x, norm_g, w_in, a_q_g, a_k_g, a_lam, a_subln_g, b_q_g, b_k_g, b_cmp_pos, b_phi_w1,
           b_phi_w2, c_w, c_scale, w_branch, w_out):
    batch, seq, d = x.shape
    x2d = x.reshape(batch * seq, d)
    ovt = _overlap_t(seq)
    two = lambda v: jnp.concatenate([v, v]).reshape(1, LANES).astype(F32)
    for l in range(DEPTH):
        z = _inproj(x2d, norm_g[l].reshape(1, d), _permute_w_in(w_in[l].astype(BF16)))
        lam_init = 0.8 - 0.6 * math.exp(-0.3 * l)
        oa = _diff_attn(z, a_lam[l].astype(F32), two(a_q_g[l]), two(a_k_g[l]),
                        a_subln_g[l].reshape(1, LANES).astype(F32),
                        batch=batch, seq=seq, lam_init=lam_init)
        kc, vc = _compress(z, *_compress_weights(b_cmp_pos[l], b_phi_w1[l], b_phi_w2[l]),
                           two(b_k_g[l, 0]), batch=batch, seq=seq)
        ob = _nsa(z, kc, vc, ovt, two(b_q_g[l]), two(b_k_g[l, 1]), two(b_k_g[l, 2]),
                  batch=batch, seq=seq)
        x2d = _merge(x2d, z, oa, ob, c_w[l].astype(BF16), c_scale[l].reshape(1, BRANCH_WIDTH),
                     (w_branch[l] * 0.5).astype(BF16), w_out[l].astype(BF16), seq=seq)
    return x2d.reshape(batch, seq, d)
```

```python
import functools
import math

import numpy as np
import jax
import jax.numpy as jnp
from jax import lax
from jax.experimental import pallas as pl
from jax.experimental.pallas import tpu as pltpu

F32 = jnp.float32
BF16 = jnp.bfloat16

D_MODEL = 1024
DEPTH = 2
BRANCH_WIDTH = D_MODEL // 2
A_DK = 64
A_DV = 2 * A_DK
A_HEADS = BRANCH_WIDTH // A_DV
B_DK = 64
B_HEADS = BRANCH_WIDTH // B_DK
B_GROUPS = 2
B_HPG = B_HEADS // B_GROUPS
CMP_LEN = 32
CMP_STRIDE = 16
SLC_LEN = 64
SLC_TOPK = 16
WIN = 512
PHI_HIDDEN = 256
N_NSA_BRANCH = 3
C_WINDOWS = (2, 4, 8, 16)
C_GDIM = BRANCH_WIDTH // len(C_WINDOWS)
N_BRANCH = 3
EPS = 1e-6
NEG_INF = -1e30
BIG = 1e30

LANES = 128
HALO = 16
ONES_ROWS = 16
VMEM_LIMIT = 48 * 1024 * 1024

ZW = 8192
BLK_AQ, BLK_AK, BLK_AV, BLK_AG = 0, 4, 8, 12
BLK_BQ, BLK_BG, BLK_CU, BLK_CG = 16, 20, 24, 28
BLK_MG = 32
BLK_BKV = 56
BLK_BBG = 62

SCALE = 0.125
QSCALE = SCALE * math.log2(math.e)


def _permute_w_in(wt):
    o = np.cumsum([0, 512, 512, 512, 512, 512, 768, 24, 512, 512, 512, 3072])
    seg = lambda i: wt[int(o[i]):int(o[i + 1])]
    aq, ak, av, ag, bq, bkv, bbg, bg, cu, cg, mg = [seg(i) for i in range(11)]
    parts = [aq, ak, av, ag * 0.5, bq, bg * 0.5, cu, cg * 0.5, mg * 0.5, bkv, bbg]
    used = sum(p.shape[0] for p in parts)
    parts.append(jnp.zeros((ZW - used, wt.shape[1]), wt.dtype))
    return jnp.concatenate(parts, axis=0)


def _sigmoid(x):
    return 0.5 * jnp.tanh(0.5 * x) + 0.5


def _half_norm(x, gain, lo):
    x2 = x * x
    s_lo = jnp.sum(jnp.where(lo, x2, 0.0), axis=-1, keepdims=True)
    s_hi = jnp.sum(jnp.where(lo, 0.0, x2), axis=-1, keepdims=True)
    ms = jnp.where(lo, s_lo, s_hi) * (1.0 / 64.0)
    return x * lax.rsqrt(ms + EPS) * gain


def _inproj_kernel(x_ref, g_ref, w_ref, o_ref, h_ref):
    @pl.when(pl.program_id(1) == 0)
    def _():
        x = x_ref[...]
        ms = jnp.mean(x * x, axis=-1, keepdims=True)
        h_ref[...] = (x * lax.rsqrt(ms + EPS) * g_ref[...]).astype(BF16)

    o_ref[...] = lax.dot_general(h_ref[...], w_ref[...], (((1,), (1,)), ((), ())),
                                 preferred_element_type=F32).astype(o_ref.dtype)


def _inproj(x2d, g, wt):
    rows, d = x2d.shape
    tm, tn = min(1024, rows), 2048
    return pl.pallas_call(
        _inproj_kernel,
        grid=(rows // tm, ZW // tn),
        in_specs=[pl.BlockSpec((tm, d), lambda i, j: (i, 0)),
                  pl.BlockSpec((1, d), lambda i, j: (0, 0)),
                  pl.BlockSpec((tn, d), lambda i, j: (j, 0))],
        out_specs=pl.BlockSpec((tm, tn), lambda i, j: (i, j)),
        out_shape=jax.ShapeDtypeStruct((rows, ZW), BF16),
        scratch_shapes=[pltpu.VMEM((tm, d), BF16)],
        compiler_params=pltpu.CompilerParams(
            dimension_semantics=("parallel", "arbitrary"), vmem_limit_bytes=VMEM_LIMIT),
        name="inproj",
    )(x2d, g, wt)


def _reset(m_ref, l_ref, acc_ref):
    m_ref[...] = jnp.full(m_ref.shape, NEG_INF, F32)
    l_ref[...] = jnp.zeros(l_ref.shape, F32)
    acc_ref[...] = jnp.zeros(acc_ref.shape, F32)


def _softmax_step_t(s, vt_tile, m_ref, l_ref, acc_ref):
    m_old = m_ref[...]
    m_new = jnp.maximum(m_old, jnp.max(s, axis=0, keepdims=True))
    alpha = jnp.exp2(m_old - m_new)
    p = jnp.exp2(s - m_new)
    l_ref[...] = alpha * l_ref[...] + jnp.sum(p, axis=0, keepdims=True)
    acc_ref[...] = alpha * acc_ref[...] + jnp.dot(vt_tile, p.astype(BF16),
                                                  preferred_element_type=F32)
    m_ref[...] = m_new


def _causal_tiles_t(nfull, chains):
    def issue(j, into_b):
        for scores_fn, _, _, sa_ref, sb_ref, _ in chains:
            (sb_ref if into_b else sa_ref)[...] = scores_fn(j)

    def consume(j, from_b, masked=False):
        for _, value_fn, mask_fn, sa_ref, sb_ref, state in chains:
            s = (sb_ref if from_b else sa_ref)[...]
            _softmax_step_t(mask_fn(s) if masked else s, value_fn(j), *state)

    def two_tiles(j):
        issue(j + 1, True)
        consume(j, False)
        issue(j + 2, False)
        consume(j + 1, True)

    issue(0, False)
    done = 0
    for per_trip in (8, 4, 2):
        def trip(i, carry, per_trip=per_trip, done=done):
            for t in range(0, per_trip, 2):
                two_tiles(done + per_trip * i + t)
            return carry
        ntrip = (nfull - done) // per_trip
        lax.fori_loop(0, ntrip, trip, 0)
        done = done + per_trip * ntrip
    odd = (nfull & 1) == 1

    @pl.when(odd)
    def _():
        issue(nfull, True)
        consume(nfull - 1, False)
        consume(nfull, True, masked=True)

    @pl.when(jnp.logical_not(odd))
    def _():
        consume(nfull, False, masked=True)


def _transpose_values(v_ref, vt_ref, seq, chunk=512):
    chunk = min(chunk, seq)

    def body(c, carry):
        r = pl.ds(pl.multiple_of(c * chunk, chunk), chunk)
        vt_ref[:, r] = v_ref[r, :].astype(F32).T.astype(BF16)
        return carry
    lax.fori_loop(0, seq // chunk, body, 0)


def _normalize_keys(src_ref, dst_ref, gain, lo, seq, chunk=512):
    def body(c, carry):
        r = pl.ds(pl.multiple_of(c * chunk, chunk), chunk)
        dst_ref[r, :] = _half_norm(src_ref[r, :].astype(F32), gain, lo).astype(BF16)
        return carry
    lax.fori_loop(0, seq // chunk, body, 0)


A_HEADS_PER_STEP = 1


def _diff_attn_kernel(lam_ref, q_ref, k_ref, v_ref, qg_ref, kg_ref, sg_ref, o_ref, *scratch,
                      tq, tk, seq, lam_init):
    qi = pl.program_id(2)
    lo = lax.broadcasted_iota(jnp.int32, (1, LANES), 1) < 64
    per_head = [scratch[7 * h:7 * h + 7] for h in range(A_HEADS_PER_STEP)]

    @pl.when(qi == 0)
    def _():
        for h, (kn_ref, vt_ref, *_) in enumerate(per_head):
            lanes = slice(h * LANES, (h + 1) * LANES)
            _normalize_keys(k_ref.at[:, lanes], kn_ref, kg_ref[...], lo, seq)
            _transpose_values(v_ref.at[:, lanes], vt_ref, seq)

    q0 = qi * tq
    nfull = q0 // tk
    top = lax.broadcasted_iota(jnp.int32, (LANES, 1), 0) < 64

    def rows_of(j):
        return pl.ds(pl.multiple_of(j * tk, tk), tk)

    def causal(s):
        qpos = q0 + lax.broadcasted_iota(jnp.int32, (tk, tq), 1)
        kpos = nfull * tk + lax.broadcasted_iota(jnp.int32, (tk, tq), 0)
        bias = jnp.where(kpos <= qpos, 0.0, NEG_INF)
        return s + jnp.concatenate([bias, bias], axis=1)

    def chain(h):
        kn_ref, vt_ref, sa_ref, sb_ref, m_ref, l_ref, acc_ref = per_head[h]
        q = q_ref[:, h * LANES:(h + 1) * LANES].astype(F32)
        qt = (_half_norm(q, qg_ref[...], lo) * QSCALE).T
        qq = jnp.concatenate([jnp.where(top, qt, 0.0), jnp.where(top, 0.0, qt)],
                             axis=1).astype(BF16)
        _reset(m_ref, l_ref, acc_ref)
        return (lambda j: jnp.dot(kn_ref[rows_of(j), :], qq, preferred_element_type=F32),
                lambda j: vt_ref[:, rows_of(j)],
                causal, sa_ref, sb_ref, (m_ref, l_ref, acc_ref))

    _causal_tiles_t(nfull, [chain(h) for h in range(A_HEADS_PER_STEP)])

    lp = lam_ref[...]
    lam = (jnp.exp(jnp.sum(lp[0:1] * lp[1:2], axis=-1, keepdims=True))
           - jnp.exp(jnp.sum(lp[2:3] * lp[3:4], axis=-1, keepdims=True)) + lam_init)
    for h, (_, _, _, _, _, l_ref, acc_ref) in enumerate(per_head):
        acc = acc_ref[...]
        l = l_ref[...]
        o = acc[:, :tq] / l[:, :tq] - lam * (acc[:, tq:] / l[:, tq:])
        ms = jnp.mean(o * o, axis=0, keepdims=True)
        o_ref[:, h * LANES:(h + 1) * LANES] = ((o * lax.rsqrt(ms + EPS)).T * sg_ref[...]
                                               * (1.0 - lam_init))


def _diff_attn(z, lam_p, qg, kg, sg, *, batch, seq, lam_init):
    tq = min(512, seq)
    tk = min(512, seq)
    nq = seq // tq
    hps = A_HEADS_PER_STEP
    wide = hps * LANES
    kern = functools.partial(_diff_attn_kernel, tq=tq, tk=tk, seq=seq, lam_init=lam_init)
    vec = lambda: pl.BlockSpec((1, LANES), lambda b, h, i: (0, 0))
    head_scratch = [pltpu.VMEM((seq, LANES), BF16),
                    pltpu.VMEM((LANES, seq), BF16),
                    pltpu.VMEM((tk, 2 * tq), F32),
                    pltpu.VMEM((tk, 2 * tq), F32),
                    pltpu.VMEM((1, 2 * tq), F32),
                    pltpu.VMEM((1, 2 * tq), F32),
                    pltpu.VMEM((LANES, 2 * tq), F32)]
    return pl.pallas_call(
        kern,
        grid=(batch, A_HEADS // hps, nq),
        in_specs=[pl.BlockSpec((4, A_DK), lambda b, h, i: (0, 0)),
                  pl.BlockSpec((tq, wide), lambda b, h, i: (b * nq + i, BLK_AQ // hps + h)),
                  pl.BlockSpec((seq, wide), lambda b, h, i: (b, BLK_AK // hps + h)),
                  pl.BlockSpec((seq, wide), lambda b, h, i: (b, BLK_AV // hps + h)),
                  vec(), vec(), vec()],
        out_specs=pl.BlockSpec((tq, wide), lambda b, h, i: (b * nq + i, h)),
        out_shape=jax.ShapeDtypeStruct((batch * seq, BRANCH_WIDTH), F32),
        scratch_shapes=head_scratch * hps,
        compiler_params=pltpu.CompilerParams(
            dimension_semantics=("parallel", "parallel", "arbitrary"),
            vmem_limit_bytes=VMEM_LIMIT),
        name="diff_attn",
    )(lam_p, z, z, z, qg, kg, sg)


def _compress_kernel(z_ref, pos_ref, w1_ref, w2_ref, g_ref, kc_ref, vct_ref, zf_ref, x_ref,
                     *, seq):
    t_kv = pl.program_id(1)
    ncp = seq // CMP_STRIDE
    lo = lax.broadcasted_iota(jnp.int32, (1, LANES), 1) < 64
    zf_ref[0:seq, :] = z_ref[...].astype(F32)
    zf_ref[seq:, :] = jnp.zeros((CMP_STRIDE, LANES), F32)
    for t in range(CMP_LEN):
        rows = zf_ref[pl.ds(t, ncp, stride=CMP_STRIDE), :]
        x_ref[:, t * LANES:(t + 1) * LANES] = (rows + pos_ref[t:t + 1, :]).astype(BF16)
    h = jnp.dot(x_ref[...], w1_ref[...], preferred_element_type=F32)
    h = h * _sigmoid(h)
    y = jnp.dot(h.astype(BF16), w2_ref[...], preferred_element_type=F32)

    @pl.when(t_kv == 0)
    def _():
        kc_ref[...] = _half_norm(y, g_ref[...], lo).astype(BF16)

    @pl.when(t_kv != 0)
    def _():
        vct_ref[0:LANES, :] = y.T.astype(BF16)
        vct_ref[LANES:, :] = jnp.ones((ONES_ROWS, ncp), BF16)


def _compress(z, pos, w1, w2, gain, *, batch, seq):
    ncp = seq // CMP_STRIDE
    wide = CMP_LEN * LANES
    return pl.pallas_call(
        functools.partial(_compress_kernel, seq=seq),
        grid=(batch, 2),
        in_specs=[pl.BlockSpec((seq, LANES), lambda b, t: (b, BLK_BKV + t)),
                  pl.BlockSpec((None, CMP_LEN, LANES), lambda b, t: (t, 0, 0)),
                  pl.BlockSpec((None, wide, 2 * PHI_HIDDEN), lambda b, t: (t, 0, 0)),
                  pl.BlockSpec((None, 2 * PHI_HIDDEN, LANES), lambda b, t: (t, 0, 0)),
                  pl.BlockSpec((1, LANES), lambda b, t: (0, 0))],
        out_specs=[pl.BlockSpec((None, ncp, LANES), lambda b, t: (b, 0, 0)),
                   pl.BlockSpec((None, LANES + ONES_ROWS, ncp), lambda b, t: (b, 0, 0))],
        out_shape=[jax.ShapeDtypeStruct((batch, ncp, LANES), BF16),
                   jax.ShapeDtypeStruct((batch, LANES + ONES_ROWS, ncp), BF16)],
        scratch_shapes=[pltpu.VMEM((seq + CMP_STRIDE, LANES), F32),
                        pltpu.VMEM((ncp, wide), BF16)],
        compiler_params=pltpu.CompilerParams(
            dimension_semantics=("arbitrary", "arbitrary"), vmem_limit_bytes=VMEM_LIMIT),
        name="compress",
    )(z, pos, w1, w2, gain)


def _compress_weights(pos, w1, w2):
    pos2 = jnp.concatenate([pos, pos], axis=-1).astype(F32)
    w1r = w1.reshape(2, CMP_LEN, B_DK, PHI_HIDDEN)
    z1 = jnp.zeros_like(w1r)
    w1b = jnp.concatenate([jnp.concatenate([w1r, z1], axis=-1),
                           jnp.concatenate([z1, w1r], axis=-1)], axis=2)
    z2 = jnp.zeros_like(w2)
    w2b = jnp.concatenate([jnp.concatenate([w2, z2], axis=-1),
                           jnp.concatenate([z2, w2], axis=-1)], axis=1)
    return pos2, w1b.reshape(2, CMP_LEN * LANES, 2 * PHI_HIDDEN).astype(BF16), w2b.astype(BF16)


def _split2_bf16(x):
    hi = x.astype(BF16)
    return hi, (x - hi.astype(F32)).astype(BF16)


def _not_selected_t(val, k):
    nb, nqry = val.shape
    kf = float(k)
    sub = lax.broadcasted_iota(jnp.int32, (8, nqry), 0)

    def over_blocks(xs, op):
        xs = list(xs)
        while len(xs) > 1:
            xs = [op(xs[i], xs[i + 1]) for i in range(0, len(xs) - 1, 2)] + xs[len(xs) & ~1:]
        acc = xs[0]
        for sh in (4, 2, 1):
            acc = op(acc, pltpu.roll(acc, sh, 0))
        return acc

    vals = [val[8 * r:8 * r + 8] for r in range(nb // 8)]
    rem = vals
    thr = jnp.full((8, nqry), -jnp.inf, F32)
    done = jnp.zeros((8, nqry), F32)
    cnt = jnp.zeros((8, nqry), F32)
    for _ in range(k):
        top = over_blocks(rem, jnp.maximum)
        hit = [x == top for x in rem]
        cnt = cnt + over_blocks([jnp.where(h, 1.0, 0.0) for h in hit], jnp.add)
        rem = [jnp.where(h, -jnp.inf, x) for h, x in zip(hit, rem)]
        thr = jnp.where(done > 0.0, thr, top)
        done = jnp.where(cnt >= kf, 1.0, done)
    above = [jnp.where(v > thr, 1.0, 0.0) for v in vals]
    need = kf - over_blocks(above, jnp.add)
    offset = jnp.zeros((8, nqry), F32)
    out = []
    for v, ab in zip(vals, above):
        tie = jnp.where(v == thr, 1.0, 0.0)
        scan = tie
        for sh in (1, 2, 4):
            scan = scan + jnp.where(sub >= sh, pltpu.roll(scan, sh, 0), 0.0)
        keep = ab + tie * jnp.where(offset + scan <= need, 1.0, 0.0)
        out.append(1.0 - keep)
        offset = offset + over_blocks([tie], jnp.add)
    return jnp.concatenate(out, axis=0)


def _nsa_kernel(q_ref, kc_ref, vct_ref, ks_ref, vs_ref, kw_ref, vw_ref, gt_ref, ovt_ref,
                qg_ref, ksg_ref, kwg_ref, o_ref,
                ksn_ref, kwn_ref, vst_ref, vwt_ref, sa_ref, sb_ref, m_ref, l_ref, acc_ref,
                part_ref, gsel_ref, *, tq, tk, seq):
    qi = pl.program_id(1)
    g = pl.program_id(2)
    ns = seq // SLC_LEN
    ncp = seq // CMP_STRIDE
    rows = B_HPG * tq
    half = LANES // 2
    lo = lax.broadcasted_iota(jnp.int32, (1, LANES), 1) < half
    g0 = (jnp.zeros((half, tq), jnp.int32) + g) == 0
    q0 = qi * tq

    @pl.when((qi == 0) & (g == 0))
    def _():
        _normalize_keys(kw_ref, kwn_ref, kwg_ref[...], lo, seq)
        _transpose_values(vs_ref, vst_ref, seq)
        _transpose_values(vw_ref, vwt_ref.at[0:LANES, :], seq)
        vwt_ref[LANES:, :] = jnp.ones((ONES_ROWS, seq), BF16)
        chunk = min(512, seq)

        def body(c, carry):
            r = pl.ds(pl.multiple_of(c * chunk, chunk), chunk)
            ksn_ref[r, 0:LANES] = _half_norm(ks_ref[r, :].astype(F32), ksg_ref[...], lo).astype(BF16)
            kblk = (c * chunk + lax.broadcasted_iota(jnp.int32, (chunk, LANES), 0)) // SLC_LEN
            own = lax.broadcasted_iota(jnp.int32, (chunk, LANES), 1) == kblk
            ksn_ref[r, LANES:2 * LANES] = jnp.where(own, NEG_INF, 0.0).astype(BF16)
            return carry
        lax.fori_loop(0, seq // chunk, body, 0)

    zeros = jnp.zeros((half, tq), F32)
    heads = []
    for pair in range(B_HPG // 2):
        xt = (_half_norm(q_ref[:, pair * LANES:(pair + 1) * LANES].astype(F32), qg_ref[...], lo)
              * QSCALE).T
        for hh in range(2):
            hrows = xt[hh * half:(hh + 1) * half]
            heads.append(jnp.concatenate([jnp.where(g0, hrows, zeros),
                                          jnp.where(g0, zeros, hrows)], axis=0))
    qs = jnp.concatenate(heads, axis=1).astype(BF16)
    qpos = q0 + (lax.broadcasted_iota(jnp.int32, (1, rows), 1) & (tq - 1))
    qpos1 = q0 + lax.broadcasted_iota(jnp.int32, (1, tq), 1)

    def per_head(bias):
        return jnp.concatenate([bias] * B_HPG, axis=1)

    gates = _sigmoid(gt_ref[...].astype(F32)).T
    g0row = (jnp.zeros((1, tq), jnp.int32) + g) == 0

    def gate_row(br):
        per_head = []
        for h in range(B_HPG):
            c = h * N_NSA_BRANCH + br
            c1 = B_HPG * N_NSA_BRANCH + c
            per_head.append(jnp.where(g0row, gates[c:c + 1], gates[c1:c1 + 1]))
        return jnp.concatenate(per_head, axis=1)

    s = jnp.dot(kc_ref[...], qs, preferred_element_type=F32)
    cmp_end = lax.broadcasted_iota(jnp.int32, (ncp, 1), 0) * CMP_STRIDE + (CMP_LEN - 1)
    s = s + per_head(jnp.where(cmp_end <= qpos1, 0.0, NEG_INF))
    e = jnp.exp2(s - jnp.max(s, axis=0, keepdims=True))
    res = jnp.dot(vct_ref[...], e.astype(BF16), preferred_element_type=F32)
    inv = jnp.where(qpos >= CMP_LEN - 1, 1.0 / res[LANES:LANES + 1], 0.0)
    p_cmp = e * inv
    o_cmp = res[0:LANES] * inv

    nk = WIN + tq
    start = pl.multiple_of(jnp.maximum(q0 - WIN, 0), tq)
    kpos = start + lax.broadcasted_iota(jnp.int32, (nk, 1), 0)
    sw = jnp.dot(kwn_ref[pl.ds(start, nk), :], qs, preferred_element_type=F32)
    sw = sw + per_head(jnp.where((kpos <= qpos1) & (kpos > qpos1 - WIN), 0.0, NEG_INF))
    pw = jnp.exp2(sw - jnp.max(sw, axis=0, keepdims=True))
    res = jnp.dot(vwt_ref[:, pl.ds(start, nk)], pw.astype(BF16), preferred_element_type=F32)
    o_win = res[0:LANES] / res[LANES:LANES + 1]

    part_ref[...] = gate_row(0) * o_cmp + gate_row(2) * o_win
    gsel_ref[...] = gate_row(1)

    psum = p_cmp[:, 0:tq]
    for h in range(1, B_HPG):
        psum = psum + p_cmp[:, h * tq:(h + 1) * tq]
    ovt = ovt_ref[...]
    imp_t = sum(jnp.dot(ovt, part, preferred_element_type=F32) for part in _split2_bf16(psum))
    blk = lax.broadcasted_iota(jnp.int32, (ns, tq), 0)
    cur = (q0 + lax.broadcasted_iota(jnp.int32, (ns, tq), 1)) // SLC_LEN
    val = jnp.where((blk == 0) | (blk == cur) | (blk == cur - 1), BIG, imp_t)
    val = jnp.where(blk > cur, NEG_INF, val)
    notsel = _not_selected_t(val, min(SLC_TOPK, ns))
    if ns < LANES:
        notsel = jnp.concatenate([notsel, jnp.zeros((LANES - ns, tq), F32)], axis=0)
    qa = jnp.concatenate([qs, jnp.concatenate([notsel.astype(BF16)] * B_HPG, axis=1)], axis=0)

    _reset(m_ref, l_ref, acc_ref)
    nfull = q0 // tk

    def rows_of(j):
        return pl.ds(pl.multiple_of(j * tk, tk), tk)

    def causal(sc):
        kpos = nfull * tk + lax.broadcasted_iota(jnp.int32, (tk, 1), 0)
        return sc + per_head(jnp.where(kpos <= qpos1, 0.0, NEG_INF))

    _causal_tiles_t(nfull, [(
        lambda j: jnp.dot(ksn_ref[rows_of(j), :], qa, preferred_element_type=F32),
        lambda j: vst_ref[:, rows_of(j)],
        causal, sa_ref, sb_ref, (m_ref, l_ref, acc_ref))])
    o = part_ref[...] + gsel_ref[...] * (acc_ref[...] / l_ref[...])
    out_t = jnp.concatenate(
        [jnp.where(g0, o[0:half, h * tq:(h + 1) * tq], o[half:LANES, h * tq:(h + 1) * tq])
         for h in range(B_HPG)], axis=0)
    o_ref[...] = out_t.T


def _nsa(z, kc, vc, ovt, qg, ksg, kwg, *, batch, seq):
    tq = min(256, seq)
    tk = min(512, seq)
    nq = seq // tq
    ns = seq // SLC_LEN
    ncp = seq // CMP_STRIDE
    rows = B_HPG * tq
    kern = functools.partial(_nsa_kernel, tq=tq, tk=tk, seq=seq)
    kv = lambda blk: pl.BlockSpec((seq, LANES), lambda b, i, g: (b, blk))
    vec = lambda: pl.BlockSpec((1, LANES), lambda b, i, g: (0, 0))
    assert seq >= WIN + tq and WIN % tq == 0 and tk % tq == 0
    return pl.pallas_call(
        kern,
        grid=(batch, nq, B_GROUPS),
        in_specs=[pl.BlockSpec((tq, 2 * LANES), lambda b, i, g: (b * nq + i, BLK_BQ // 2 + g)),
                  pl.BlockSpec((None, ncp, LANES), lambda b, i, g: (b, 0, 0)),
                  pl.BlockSpec((None, LANES + ONES_ROWS, ncp), lambda b, i, g: (b, 0, 0)),
                  kv(BLK_BKV + 2), kv(BLK_BKV + 3), kv(BLK_BKV + 4), kv(BLK_BKV + 5),
                  pl.BlockSpec((tq, LANES), lambda b, i, g: (b * nq + i, BLK_BBG)),
                  pl.BlockSpec((ns, ncp), lambda b, i, g: (0, 0)),
                  vec(), vec(), vec()],
        out_specs=pl.BlockSpec((tq, 2 * LANES), lambda b, i, g: (b * nq + i, g)),
        out_shape=jax.ShapeDtypeStruct((batch * seq, BRANCH_WIDTH), F32),
        scratch_shapes=[pltpu.VMEM((seq, 2 * LANES), BF16),
                        pltpu.VMEM((seq, LANES), BF16),
                        pltpu.VMEM((LANES, seq), BF16),
                        pltpu.VMEM((LANES + ONES_ROWS, seq), BF16),
                        pltpu.VMEM((tk, rows), F32),
                        pltpu.VMEM((tk, rows), F32),
                        pltpu.VMEM((1, rows), F32),
                        pltpu.VMEM((1, rows), F32),
                        pltpu.VMEM((LANES, rows), F32),
                        pltpu.VMEM((LANES, rows), F32),
                        pltpu.VMEM((1, rows), F32)],
        compiler_params=pltpu.CompilerParams(
            dimension_semantics=("parallel", "arbitrary", "arbitrary"),
            vmem_limit_bytes=VMEM_LIMIT),
        name="nsa",
    )(z, kc, vc, z, z, z, z, z, ovt, qg, ksg, kwg)


def _merge_kernel(x_ref, oa_ref, ob_ref, ag_ref, bg_ref, cu_ref, halo_ref, cg_ref,
                  mg0_ref, mg1_ref, mg2_ref, cw_ref, cs_ref, wb_ref, wo_ref, o_ref,
                  ext_ref, *, tm, seq):
    i = pl.program_id(0)
    tpos0 = (i * tm) % seq

    ext_ref[HALO:, :] = cu_ref[...].astype(F32)
    ext_ref[:HALO, :] = jnp.where(tpos0 == 0, 0.0, halo_ref[...].astype(F32))
    tpos = tpos0 + lax.broadcasted_iota(jnp.int32, (tm, 1), 0)
    pooled = []
    for gi, w in enumerate(C_WINDOWS):
        c = slice(gi * C_GDIM, (gi + 1) * C_GDIM)
        cur = ext_ref[HALO:, c]
        tot = cur
        for d in range(1, w):
            tot = tot + ext_ref[HALO - d:HALO - d + tm, c]
        cnt = jnp.minimum(tpos + 1, w).astype(F32)
        pg = (tot / cnt - cur).astype(BF16)
        pooled.append(jnp.dot(pg, cw_ref[gi], preferred_element_type=F32))
    oc = jnp.concatenate(pooled, axis=1) * cs_ref[...]

    def gated(o, gate_ref, k):
        u = gate_ref[...].astype(F32)
        return jnp.dot((o * (u + u * jnp.tanh(u))).astype(BF16), wb_ref[k],
                       preferred_element_type=F32)

    def merge_gate(mg_ref, y):
        return y + jnp.tanh(mg_ref[...].astype(F32)) * y

    merged = (merge_gate(mg0_ref, gated(oa_ref[...], ag_ref, 0))
              + merge_gate(mg1_ref, gated(ob_ref[...], bg_ref, 1))
              + merge_gate(mg2_ref, gated(oc, cg_ref, 2)))
    o_ref[...] = x_ref[...] + jnp.dot(merged.astype(BF16), wo_ref[...], preferred_element_type=F32)


def _merge(x2d, z, oa, ob, cw, cs, wb, wo, *, seq):
    rows, d = x2d.shape
    tm = min(512, seq)
    bw = BRANCH_WIDTH
    kern = functools.partial(_merge_kernel, tm=tm, seq=seq)
    zcol = lambda blk: pl.BlockSpec((tm, bw), lambda i: (i, blk * LANES // bw))
    mgs = lambda k: pl.BlockSpec((tm, d), lambda i: (i, BLK_MG * LANES // d + k))
    full = lambda a: pl.BlockSpec(a.shape, lambda i: (0,) * a.ndim)
    return pl.pallas_call(
        kern,
        grid=(rows // tm,),
        in_specs=[pl.BlockSpec((tm, d), lambda i: (i, 0)),
                  pl.BlockSpec((tm, bw), lambda i: (i, 0)),
                  pl.BlockSpec((tm, bw), lambda i: (i, 0)),
                  zcol(BLK_AG), zcol(BLK_BG), zcol(BLK_CU),
                  pl.BlockSpec((HALO, bw),
                               lambda i: (jnp.maximum(i * (tm // HALO) - 1, 0), BLK_CU * LANES // bw)),
                  zcol(BLK_CG), mgs(0), mgs(1), mgs(2),
                  full(cw), full(cs), full(wb), full(wo)],
        out_specs=pl.BlockSpec((tm, d), lambda i: (i, 0)),
        out_shape=jax.ShapeDtypeStruct((rows, d), F32),
        scratch_shapes=[pltpu.VMEM((tm + HALO, bw), F32)],
        compiler_params=pltpu.CompilerParams(
            dimension_semantics=("parallel",), vmem_limit_bytes=VMEM_LIMIT),
        name="merge",
    )(x2d, oa, ob, z, z, z, z, z, z, z, z, cw, cs, wb, wo)


def _overlap_t(seq):
    ncp = seq // CMP_STRIDE
    nc = (seq - CMP_LEN) // CMP_STRIDE + 1
    ns = seq // SLC_LEN
    cs = np.arange(ncp) * CMP_STRIDE
    ce = cs + CMP_LEN
    ss = np.arange(ns) * SLC_LEN
    se = ss + SLC_LEN
    ov = np.clip(np.minimum(ce[None, :], se[:, None]) - np.maximum(cs[None, :], ss[:, None]), 0, None)
    ov = ov / CMP_LEN
    ov[:, nc:] = 0.0
    return jnp.asarray(ov, dtype=BF16)


def kernel(x, norm_g, w_in, a_q_g, a_k_g, a_lam, a_subln_g, b_q_g, b_k_g, b_cmp_pos, b_phi_w1,
           b_phi_w2, c_w, c_scale, w_branch, w_out):
    batch, seq, d = x.shape
    x2d = x.reshape(batch * seq, d)
    ovt = _overlap_t(seq)
    two = lambda v: jnp.concatenate([v, v]).reshape(1, LANES).astype(F32)
    for l in range(DEPTH):
        z = _inproj(x2d, norm_g[l].reshape(1, d), _permute_w_in(w_in[l].T.astype(BF16)))
        lam_init = 0.8 - 0.6 * math.exp(-0.3 * l)
        oa = _diff_attn(z, a_lam[l].astype(F32), two(a_q_g[l]), two(a_k_g[l]),
                        a_subln_g[l].reshape(1, LANES).astype(F32),
                        batch=batch, seq=seq, lam_init=lam_init)
        kc, vc = _compress(z, *_compress_weights(b_cmp_pos[l], b_phi_w1[l], b_phi_w2[l]),
                           two(b_k_g[l, 0]), batch=batch, seq=seq)
        ob = _nsa(z, kc, vc, ovt, two(b_q_g[l]), two(b_k_g[l, 1]), two(b_k_g[l, 2]),
                  batch=batch, seq=seq)
        x2d = _merge(x2d, z, oa, ob, c_w[l].astype(BF16), c_scale[l].reshape(1, BRANCH_WIDTH),
                     (w_branch[l] * 0.5).astype(BF16), w_out[l].astype(BF16), seq=seq)
    return x2d.reshape(batch, seq, d)
```

```python
import functools
import math

import numpy as np
import jax
import jax.numpy as jnp
from jax import lax
from jax.experimental import pallas as pl
from jax.experimental.pallas import tpu as pltpu

F32 = jnp.float32
BF16 = jnp.bfloat16

D_MODEL = 1024
DEPTH = 2
BRANCH_WIDTH = D_MODEL // 2
A_DK = 64
A_DV = 2 * A_DK
A_HEADS = BRANCH_WIDTH // A_DV
B_DK = 64
B_HEADS = BRANCH_WIDTH // B_DK
B_GROUPS = 2
B_HPG = B_HEADS // B_GROUPS
CMP_LEN = 32
CMP_STRIDE = 16
SLC_LEN = 64
SLC_TOPK = 16
WIN = 512
PHI_HIDDEN = 256
N_NSA_BRANCH = 3
C_WINDOWS = (2, 4, 8, 16)
C_GDIM = BRANCH_WIDTH // len(C_WINDOWS)
N_BRANCH = 3
EPS = 1e-6
NEG_INF = -1e30
BIG = 1e30

LANES = 128
HALO = 16
ONES_ROWS = 16
PREP_ROWS = 2048
VMEM_LIMIT = 48 * 1024 * 1024

ZW = 8192
BLK_AQ, BLK_AK, BLK_AV, BLK_AG = 0, 4, 8, 12
BLK_BQ, BLK_BG, BLK_CU, BLK_CG = 16, 20, 24, 28
BLK_MG = 32
BLK_BKV = 56
BLK_BBG = 62

SCALE = 0.125
QSCALE = SCALE * math.log2(math.e)


def _permute_w_in(wt):
    o = np.cumsum([0, 512, 512, 512, 512, 512, 768, 24, 512, 512, 512, 3072])
    seg = lambda i: wt[int(o[i]):int(o[i + 1])]
    aq, ak, av, ag, bq, bkv, bbg, bg, cu, cg, mg = [seg(i) for i in range(11)]
    parts = [aq, ak, av, ag * 0.5, bq, bg * 0.5, cu, cg * 0.5, mg * 0.5, bkv, bbg]
    used = sum(p.shape[0] for p in parts)
    parts.append(jnp.zeros((ZW - used, wt.shape[1]), wt.dtype))
    return jnp.concatenate(parts, axis=0)


def _sigmoid(x):
    return 0.5 * jnp.tanh(0.5 * x) + 0.5


def _half_norm(x, gain, lo):
    x2 = x * x
    s_lo = jnp.sum(jnp.where(lo, x2, 0.0), axis=-1, keepdims=True)
    s_hi = jnp.sum(jnp.where(lo, 0.0, x2), axis=-1, keepdims=True)
    ms = jnp.where(lo, s_lo, s_hi) * (1.0 / 64.0)
    return x * lax.rsqrt(ms + EPS) * gain


def _inproj_kernel(x_ref, g_ref, w_ref, o_ref, h_ref):
    @pl.when(pl.program_id(1) == 0)
    def _():
        x = x_ref[...]
        ms = jnp.mean(x * x, axis=-1, keepdims=True)
        h_ref[...] = (x * lax.rsqrt(ms + EPS) * g_ref[...]).astype(BF16)

    o_ref[...] = lax.dot_general(h_ref[...], w_ref[...], (((1,), (1,)), ((), ())),
                                 preferred_element_type=F32).astype(o_ref.dtype)


def _inproj(x2d, g, wt):
    rows, d = x2d.shape
    tm, tn = min(1024, rows), 2048
    return pl.pallas_call(
        _inproj_kernel,
        grid=(rows // tm, ZW // tn),
        in_specs=[pl.BlockSpec((tm, d), lambda i, j: (i, 0)),
                  pl.BlockSpec((1, d), lambda i, j: (0, 0)),
                  pl.BlockSpec((tn, d), lambda i, j: (j, 0))],
        out_specs=pl.BlockSpec((tm, tn), lambda i, j: (i, j)),
        out_shape=jax.ShapeDtypeStruct((rows, ZW), BF16),
        scratch_shapes=[pltpu.VMEM((tm, d), BF16)],
        compiler_params=pltpu.CompilerParams(
            dimension_semantics=("parallel", "arbitrary"), vmem_limit_bytes=VMEM_LIMIT),
        name="inproj",
    )(x2d, g, wt)


def _reset(m_ref, l_ref, acc_ref):
    m_ref[...] = jnp.full(m_ref.shape, NEG_INF, F32)
    l_ref[...] = jnp.zeros(l_ref.shape, F32)
    acc_ref[...] = jnp.zeros(acc_ref.shape, F32)


def _softmax_step_t(s, vt_tile, m_ref, l_ref, acc_ref):
    m_old = m_ref[...]
    m_new = jnp.maximum(m_old, jnp.max(s, axis=0, keepdims=True))
    alpha = jnp.exp2(m_old - m_new)
    p = jnp.exp2(s - m_new)
    l_ref[...] = alpha * l_ref[...] + jnp.sum(p, axis=0, keepdims=True)
    acc_ref[...] = alpha * acc_ref[...] + jnp.dot(vt_tile, p.astype(BF16),
                                                  preferred_element_type=F32)
    m_ref[...] = m_new


def _causal_tiles_t(nfull, chains):
    def issue(j, into_b):
        for scores_fn, _, _, sa_ref, sb_ref, _ in chains:
            (sb_ref if into_b else sa_ref)[...] = scores_fn(j)

    def consume(j, from_b, masked=False):
        for _, value_fn, mask_fn, sa_ref, sb_ref, state in chains:
            s = (sb_ref if from_b else sa_ref)[...]
            _softmax_step_t(mask_fn(s) if masked else s, value_fn(j), *state)

    def two_tiles(j):
        issue(j + 1, True)
        consume(j, False)
        issue(j + 2, False)
        consume(j + 1, True)

    issue(0, False)
    done = 0
    for per_trip in (8, 4, 2):
        def trip(i, carry, per_trip=per_trip, done=done):
            for t in range(0, per_trip, 2):
                two_tiles(done + per_trip * i + t)
            return carry
        ntrip = (nfull - done) // per_trip
        lax.fori_loop(0, ntrip, trip, 0)
        done = done + per_trip * ntrip
    odd = (nfull & 1) == 1

    @pl.when(odd)
    def _():
        issue(nfull, True)
        consume(nfull - 1, False)
        consume(nfull, True, masked=True)

    @pl.when(jnp.logical_not(odd))
    def _():
        consume(nfull, False, masked=True)


def _transpose_values(v_ref, vt_ref, seq):
    chunk = min(PREP_ROWS, seq)

    def body(c, carry):
        r = pl.ds(pl.multiple_of(c * chunk, chunk), chunk)
        vt_ref[:, r] = v_ref[r, :].astype(F32).T.astype(BF16)
        return carry
    lax.fori_loop(0, seq // chunk, body, 0)


def _normalize_keys(src_ref, dst_ref, gain, lo, seq):
    chunk = min(PREP_ROWS, seq)

    def body(c, carry):
        r = pl.ds(pl.multiple_of(c * chunk, chunk), chunk)
        dst_ref[r, :] = _half_norm(src_ref[r, :].astype(F32), gain, lo).astype(BF16)
        return carry
    lax.fori_loop(0, seq // chunk, body, 0)


A_HEADS_PER_STEP = 1


def _diff_attn_kernel(lam_ref, q_ref, k_ref, v_ref, qg_ref, kg_ref, sg_ref, o_ref, *scratch,
                      tq, tk, seq, lam_init):
    qi = pl.program_id(2)
    lo = lax.broadcasted_iota(jnp.int32, (1, LANES), 1) < 64
    per_head = [scratch[7 * h:7 * h + 7] for h in range(A_HEADS_PER_STEP)]

    @pl.when(qi == 0)
    def _():
        for h, (kn_ref, vt_ref, *_) in enumerate(per_head):
            lanes = slice(h * LANES, (h + 1) * LANES)
            _normalize_keys(k_ref.at[:, lanes], kn_ref, kg_ref[...], lo, seq)
            _transpose_values(v_ref.at[:, lanes], vt_ref, seq)

    q0 = qi * tq
    nfull = q0 // tk
    top = lax.broadcasted_iota(jnp.int32, (LANES, 1), 0) < 64

    def rows_of(j):
        return pl.ds(pl.multiple_of(j * tk, tk), tk)

    def causal(s):
        qpos = q0 + lax.broadcasted_iota(jnp.int32, (tk, tq), 1)
        kpos = nfull * tk + lax.broadcasted_iota(jnp.int32, (tk, tq), 0)
        bias = jnp.where(kpos <= qpos, 0.0, NEG_INF)
        return s + jnp.concatenate([bias, bias], axis=1)

    def chain(h):
        kn_ref, vt_ref, sa_ref, sb_ref, m_ref, l_ref, acc_ref = per_head[h]
        q = q_ref[:, h * LANES:(h + 1) * LANES].astype(F32)
        qt = (_half_norm(q, qg_ref[...], lo) * QSCALE).T
        qq = jnp.concatenate([jnp.where(top, qt, 0.0), jnp.where(top, 0.0, qt)],
                             axis=1).astype(BF16)
        _reset(m_ref, l_ref, acc_ref)
        return (lambda j: jnp.dot(kn_ref[rows_of(j), :], qq, preferred_element_type=F32),
                lambda j: vt_ref[:, rows_of(j)],
                causal, sa_ref, sb_ref, (m_ref, l_ref, acc_ref))

    _causal_tiles_t(nfull, [chain(h) for h in range(A_HEADS_PER_STEP)])

    lp = lam_ref[...]
    lam = (jnp.exp(jnp.sum(lp[0:1] * lp[1:2], axis=-1, keepdims=True))
           - jnp.exp(jnp.sum(lp[2:3] * lp[3:4], axis=-1, keepdims=True)) + lam_init)
    for h, (_, _, _, _, _, l_ref, acc_ref) in enumerate(per_head):
        acc = acc_ref[...]
        l = l_ref[...]
        o = acc[:, :tq] / l[:, :tq] - lam * (acc[:, tq:] / l[:, tq:])
        ms = jnp.mean(o * o, axis=0, keepdims=True)
        o_ref[:, h * LANES:(h + 1) * LANES] = ((o * lax.rsqrt(ms + EPS)).T * sg_ref[...]
                                               * (1.0 - lam_init))


def _diff_attn(z, lam_p, qg, kg, sg, *, batch, seq, lam_init):
    tq = min(512, seq)
    tk = min(512, seq)
    nq = seq // tq
    hps = A_HEADS_PER_STEP
    wide = hps * LANES
    kern = functools.partial(_diff_attn_kernel, tq=tq, tk=tk, seq=seq, lam_init=lam_init)
    vec = lambda: pl.BlockSpec((1, LANES), lambda b, h, i: (0, 0))
    head_scratch = [pltpu.VMEM((seq, LANES), BF16),
                    pltpu.VMEM((LANES, seq), BF16),
                    pltpu.VMEM((tk, 2 * tq), F32),
                    pltpu.VMEM((tk, 2 * tq), F32),
                    pltpu.VMEM((1, 2 * tq), F32),
                    pltpu.VMEM((1, 2 * tq), F32),
                    pltpu.VMEM((LANES, 2 * tq), F32)]
    return pl.pallas_call(
        kern,
        grid=(batch, A_HEADS // hps, nq),
        in_specs=[pl.BlockSpec((4, A_DK), lambda b, h, i: (0, 0)),
                  pl.BlockSpec((tq, wide), lambda b, h, i: (b * nq + i, BLK_AQ // hps + h)),
                  pl.BlockSpec((seq, wide), lambda b, h, i: (b, BLK_AK // hps + h)),
                  pl.BlockSpec((seq, wide), lambda b, h, i: (b, BLK_AV // hps + h)),
                  vec(), vec(), vec()],
        out_specs=pl.BlockSpec((tq, wide), lambda b, h, i: (b * nq + i, h)),
        out_shape=jax.ShapeDtypeStruct((batch * seq, BRANCH_WIDTH), F32),
        scratch_shapes=head_scratch * hps,
        compiler_params=pltpu.CompilerParams(
            dimension_semantics=("parallel", "parallel", "arbitrary"),
            vmem_limit_bytes=VMEM_LIMIT),
        name="diff_attn",
    )(lam_p, z, z, z, qg, kg, sg)


def _compress_kernel(z_ref, pos_ref, w1_ref, w2_ref, g_ref, kc_ref, vct_ref, zf_ref, x_ref,
                     *, seq):
    t_kv = pl.program_id(1)
    ncp = seq // CMP_STRIDE
    lo = lax.broadcasted_iota(jnp.int32, (1, LANES), 1) < 64
    zf_ref[0:seq, :] = z_ref[...].astype(F32)
    zf_ref[seq:, :] = jnp.zeros((CMP_STRIDE, LANES), F32)
    for t in range(CMP_LEN):
        rows = zf_ref[pl.ds(t, ncp, stride=CMP_STRIDE), :]
        x_ref[:, t * LANES:(t + 1) * LANES] = (rows + pos_ref[t:t + 1, :]).astype(BF16)
    h = jnp.dot(x_ref[...], w1_ref[...], preferred_element_type=F32)
    h = h * _sigmoid(h)
    y = jnp.dot(h.astype(BF16), w2_ref[...], preferred_element_type=F32)

    @pl.when(t_kv == 0)
    def _():
        kc_ref[...] = _half_norm(y, g_ref[...], lo).astype(BF16)

    @pl.when(t_kv != 0)
    def _():
        vct_ref[0:LANES, :] = y.T.astype(BF16)
        vct_ref[LANES:, :] = jnp.ones((ONES_ROWS, ncp), BF16)


def _compress(z, pos, w1, w2, gain, *, batch, seq):
    ncp = seq // CMP_STRIDE
    wide = CMP_LEN * LANES
    return pl.pallas_call(
        functools.partial(_compress_kernel, seq=seq),
        grid=(batch, 2),
        in_specs=[pl.BlockSpec((seq, LANES), lambda b, t: (b, BLK_BKV + t)),
                  pl.BlockSpec((None, CMP_LEN, LANES), lambda b, t: (t, 0, 0)),
                  pl.BlockSpec((None, wide, 2 * PHI_HIDDEN), lambda b, t: (t, 0, 0)),
                  pl.BlockSpec((None, 2 * PHI_HIDDEN, LANES), lambda b, t: (t, 0, 0)),
                  pl.BlockSpec((1, LANES), lambda b, t: (0, 0))],
        out_specs=[pl.BlockSpec((None, ncp, LANES), lambda b, t: (b, 0, 0)),
                   pl.BlockSpec((None, LANES + ONES_ROWS, ncp), lambda b, t: (b, 0, 0))],
        out_shape=[jax.ShapeDtypeStruct((batch, ncp, LANES), BF16),
                   jax.ShapeDtypeStruct((batch, LANES + ONES_ROWS, ncp), BF16)],
        scratch_shapes=[pltpu.VMEM((seq + CMP_STRIDE, LANES), F32),
                        pltpu.VMEM((ncp, wide), BF16)],
        compiler_params=pltpu.CompilerParams(
            dimension_semantics=("arbitrary", "arbitrary"), vmem_limit_bytes=VMEM_LIMIT),
        name="compress",
    )(z, pos, w1, w2, gain)


def _compress_weights(pos, w1, w2):
    pos2 = jnp.concatenate([pos, pos], axis=-1).astype(F32)
    w1r = w1.reshape(2, CMP_LEN, B_DK, PHI_HIDDEN)
    z1 = jnp.zeros_like(w1r)
    w1b = jnp.concatenate([jnp.concatenate([w1r, z1], axis=-1),
                           jnp.concatenate([z1, w1r], axis=-1)], axis=2)
    z2 = jnp.zeros_like(w2)
    w2b = jnp.concatenate([jnp.concatenate([w2, z2], axis=-1),
                           jnp.concatenate([z2, w2], axis=-1)], axis=1)
    return pos2, w1b.reshape(2, CMP_LEN * LANES, 2 * PHI_HIDDEN).astype(BF16), w2b.astype(BF16)


def _split2_bf16(x):
    hi = x.astype(BF16)
    return hi, (x - hi.astype(F32)).astype(BF16)


def _not_selected_t(val, k):
    nb, nqry = val.shape
    kf = float(k)
    sub = lax.broadcasted_iota(jnp.int32, (8, nqry), 0)

    def over_blocks(xs, op):
        xs = list(xs)
        while len(xs) > 1:
            xs = [op(xs[i], xs[i + 1]) for i in range(0, len(xs) - 1, 2)] + xs[len(xs) & ~1:]
        acc = xs[0]
        for sh in (4, 2, 1):
            acc = op(acc, pltpu.roll(acc, sh, 0))
        return acc

    vals = [val[8 * r:8 * r + 8] for r in range(nb // 8)]
    rem = vals
    thr = jnp.full((8, nqry), -jnp.inf, F32)
    done = jnp.zeros((8, nqry), F32)
    cnt = jnp.zeros((8, nqry), F32)
    for _ in range(k):
        top = over_blocks(rem, jnp.maximum)
        hit = [x == top for x in rem]
        cnt = cnt + over_blocks([jnp.where(h, 1.0, 0.0) for h in hit], jnp.add)
        rem = [jnp.where(h, -jnp.inf, x) for h, x in zip(hit, rem)]
        thr = jnp.where(done > 0.0, thr, top)
        done = jnp.where(cnt >= kf, 1.0, done)
    above = [jnp.where(v > thr, 1.0, 0.0) for v in vals]
    need = kf - over_blocks(above, jnp.add)
    offset = jnp.zeros((8, nqry), F32)
    out = []
    for v, ab in zip(vals, above):
        tie = jnp.where(v == thr, 1.0, 0.0)
        scan = tie
        for sh in (1, 2, 4):
            scan = scan + jnp.where(sub >= sh, pltpu.roll(scan, sh, 0), 0.0)
        keep = ab + tie * jnp.where(offset + scan <= need, 1.0, 0.0)
        out.append(1.0 - keep)
        offset = offset + over_blocks([tie], jnp.add)
    return jnp.concatenate(out, axis=0)


def _nsa_kernel(q_ref, kc_ref, vct_ref, ks_ref, vs_ref, kw_ref, vw_ref, gt_ref, ovt_ref,
                qg_ref, ksg_ref, kwg_ref, o_ref,
                ksn_ref, kwn_ref, vst_ref, vwt_ref, sa_ref, sb_ref, m_ref, l_ref, acc_ref,
                part_ref, gsel_ref, *, tq, tk, seq):
    qi = pl.program_id(1)
    g = pl.program_id(2)
    ns = seq // SLC_LEN
    ncp = seq // CMP_STRIDE
    rows = B_HPG * tq
    half = LANES // 2
    lo = lax.broadcasted_iota(jnp.int32, (1, LANES), 1) < half
    g0 = (jnp.zeros((half, tq), jnp.int32) + g) == 0
    q0 = qi * tq

    @pl.when((qi == 0) & (g == 0))
    def _():
        _normalize_keys(kw_ref, kwn_ref, kwg_ref[...], lo, seq)
        _transpose_values(vs_ref, vst_ref, seq)
        _transpose_values(vw_ref, vwt_ref.at[0:LANES, :], seq)
        vwt_ref[LANES:, :] = jnp.ones((ONES_ROWS, seq), BF16)
        chunk = min(PREP_ROWS, seq)

        def body(c, carry):
            r = pl.ds(pl.multiple_of(c * chunk, chunk), chunk)
            ksn_ref[r, 0:LANES] = _half_norm(ks_ref[r, :].astype(F32), ksg_ref[...], lo).astype(BF16)
            kblk = (c * chunk + lax.broadcasted_iota(jnp.int32, (chunk, LANES), 0)) // SLC_LEN
            own = lax.broadcasted_iota(jnp.int32, (chunk, LANES), 1) == kblk
            ksn_ref[r, LANES:2 * LANES] = jnp.where(own, NEG_INF, 0.0).astype(BF16)
            return carry
        lax.fori_loop(0, seq // chunk, body, 0)

    zeros = jnp.zeros((half, tq), F32)
    heads = []
    for pair in range(B_HPG // 2):
        xt = (_half_norm(q_ref[:, pair * LANES:(pair + 1) * LANES].astype(F32), qg_ref[...], lo)
              * QSCALE).T
        for hh in range(2):
            hrows = xt[hh * half:(hh + 1) * half]
            heads.append(jnp.concatenate([jnp.where(g0, hrows, zeros),
                                          jnp.where(g0, zeros, hrows)], axis=0))
    qs = jnp.concatenate(heads, axis=1).astype(BF16)
    qpos = q0 + (lax.broadcasted_iota(jnp.int32, (1, rows), 1) & (tq - 1))
    qpos1 = q0 + lax.broadcasted_iota(jnp.int32, (1, tq), 1)

    def per_head(bias):
        return jnp.concatenate([bias] * B_HPG, axis=1)

    gates = _sigmoid(gt_ref[...].astype(F32)).T
    g0row = (jnp.zeros((1, tq), jnp.int32) + g) == 0

    def gate_row(br):
        per_head = []
        for h in range(B_HPG):
            c = h * N_NSA_BRANCH + br
            c1 = B_HPG * N_NSA_BRANCH + c
            per_head.append(jnp.where(g0row, gates[c:c + 1], gates[c1:c1 + 1]))
        return jnp.concatenate(per_head, axis=1)

    s = jnp.dot(kc_ref[...], qs, preferred_element_type=F32)
    cmp_end = lax.broadcasted_iota(jnp.int32, (ncp, 1), 0) * CMP_STRIDE + (CMP_LEN - 1)
    s = s + per_head(jnp.where(cmp_end <= qpos1, 0.0, NEG_INF))
    e = jnp.exp2(s - jnp.max(s, axis=0, keepdims=True))
    res = jnp.dot(vct_ref[...], e.astype(BF16), preferred_element_type=F32)
    inv = jnp.where(qpos >= CMP_LEN - 1, 1.0 / res[LANES:LANES + 1], 0.0)
    p_cmp = e * inv
    o_cmp = res[0:LANES] * inv

    nk = WIN + tq
    start = pl.multiple_of(jnp.maximum(q0 - WIN, 0), tq)
    kpos = start + lax.broadcasted_iota(jnp.int32, (nk, 1), 0)
    sw = jnp.dot(kwn_ref[pl.ds(start, nk), :], qs, preferred_element_type=F32)
    sw = sw + per_head(jnp.where((kpos <= qpos1) & (kpos > qpos1 - WIN), 0.0, NEG_INF))
    pw = jnp.exp2(sw - jnp.max(sw, axis=0, keepdims=True))
    res = jnp.dot(vwt_ref[:, pl.ds(start, nk)], pw.astype(BF16), preferred_element_type=F32)
    o_win = res[0:LANES] / res[LANES:LANES + 1]

    part_ref[...] = gate_row(0) * o_cmp + gate_row(2) * o_win
    gsel_ref[...] = gate_row(1)

    psum = p_cmp[:, 0:tq]
    for h in range(1, B_HPG):
        psum = psum + p_cmp[:, h * tq:(h + 1) * tq]
    ovt = ovt_ref[...]
    imp_t = sum(jnp.dot(ovt, part, preferred_element_type=F32) for part in _split2_bf16(psum))
    blk = lax.broadcasted_iota(jnp.int32, (ns, tq), 0)
    cur = (q0 + lax.broadcasted_iota(jnp.int32, (ns, tq), 1)) // SLC_LEN
    val = jnp.where((blk == 0) | (blk == cur) | (blk == cur - 1), BIG, imp_t)
    val = jnp.where(blk > cur, NEG_INF, val)
    notsel = _not_selected_t(val, min(SLC_TOPK, ns))
    if ns < LANES:
        notsel = jnp.concatenate([notsel, jnp.zeros((LANES - ns, tq), F32)], axis=0)
    qa = jnp.concatenate([qs, jnp.concatenate([notsel.astype(BF16)] * B_HPG, axis=1)], axis=0)

    _reset(m_ref, l_ref, acc_ref)
    nfull = q0 // tk

    def rows_of(j):
        return pl.ds(pl.multiple_of(j * tk, tk), tk)

    def causal(sc):
        kpos = nfull * tk + lax.broadcasted_iota(jnp.int32, (tk, 1), 0)
        return sc + per_head(jnp.where(kpos <= qpos1, 0.0, NEG_INF))

    _causal_tiles_t(nfull, [(
        lambda j: jnp.dot(ksn_ref[rows_of(j), :], qa, preferred_element_type=F32),
        lambda j: vst_ref[:, rows_of(j)],
        causal, sa_ref, sb_ref, (m_ref, l_ref, acc_ref))])
    o = part_ref[...] + gsel_ref[...] * (acc_ref[...] / l_ref[...])
    out_t = jnp.concatenate(
        [jnp.where(g0, o[0:half, h * tq:(h + 1) * tq], o[half:LANES, h * tq:(h + 1) * tq])
         for h in range(B_HPG)], axis=0)
    o_ref[...] = out_t.T


def _nsa(z, kc, vc, ovt, qg, ksg, kwg, *, batch, seq):
    tq = min(256, seq)
    tk = min(512, seq)
    nq = seq // tq
    ns = seq // SLC_LEN
    ncp = seq // CMP_STRIDE
    rows = B_HPG * tq
    kern = functools.partial(_nsa_kernel, tq=tq, tk=tk, seq=seq)
    kv = lambda blk: pl.BlockSpec((seq, LANES), lambda b, i, g: (b, blk))
    vec = lambda: pl.BlockSpec((1, LANES), lambda b, i, g: (0, 0))
    assert seq >= WIN + tq and WIN % tq == 0 and tk % tq == 0
    return pl.pallas_call(
        kern,
        grid=(batch, nq, B_GROUPS),
        in_specs=[pl.BlockSpec((tq, 2 * LANES), lambda b, i, g: (b * nq + i, BLK_BQ // 2 + g)),
                  pl.BlockSpec((None, ncp, LANES), lambda b, i, g: (b, 0, 0)),
                  pl.BlockSpec((None, LANES + ONES_ROWS, ncp), lambda b, i, g: (b, 0, 0)),
                  kv(BLK_BKV + 2), kv(BLK_BKV + 3), kv(BLK_BKV + 4), kv(BLK_BKV + 5),
                  pl.BlockSpec((tq, LANES), lambda b, i, g: (b * nq + i, BLK_BBG)),
                  pl.BlockSpec((ns, ncp), lambda b, i, g: (0, 0)),
                  vec(), vec(), vec()],
        out_specs=pl.BlockSpec((tq, 2 * LANES), lambda b, i, g: (b * nq + i, g)),
        out_shape=jax.ShapeDtypeStruct((batch * seq, BRANCH_WIDTH), F32),
        scratch_shapes=[pltpu.VMEM((seq, 2 * LANES), BF16),
                        pltpu.VMEM((seq, LANES), BF16),
                        pltpu.VMEM((LANES, seq), BF16),
                        pltpu.VMEM((LANES + ONES_ROWS, seq), BF16),
                        pltpu.VMEM((tk, rows), F32),
                        pltpu.VMEM((tk, rows), F32),
                        pltpu.VMEM((1, rows), F32),
                        pltpu.VMEM((1, rows), F32),
                        pltpu.VMEM((LANES, rows), F32),
                        pltpu.VMEM((LANES, rows), F32),
                        pltpu.VMEM((1, rows), F32)],
        compiler_params=pltpu.CompilerParams(
            dimension_semantics=("parallel", "arbitrary", "arbitrary"),
            vmem_limit_bytes=VMEM_LIMIT),
        name="nsa",
    )(z, kc, vc, z, z, z, z, z, ovt, qg, ksg, kwg)


def _merge_kernel(x_ref, oa_ref, ob_ref, ag_ref, bg_ref, cu_ref, halo_ref, cg_ref,
                  mg0_ref, mg1_ref, mg2_ref, cw_ref, cs_ref, wb_ref, wo_ref, o_ref,
                  ext_ref, *, tm, seq):
    i = pl.program_id(0)
    tpos0 = (i * tm) % seq

    ext_ref[HALO:, :] = cu_ref[...].astype(F32)
    ext_ref[:HALO, :] = jnp.where(tpos0 == 0, 0.0, halo_ref[...].astype(F32))
    tpos = tpos0 + lax.broadcasted_iota(jnp.int32, (tm, 1), 0)
    pooled = []
    for gi, w in enumerate(C_WINDOWS):
        c = slice(gi * C_GDIM, (gi + 1) * C_GDIM)
        cur = ext_ref[HALO:, c]
        tot = cur
        for d in range(1, w):
            tot = tot + ext_ref[HALO - d:HALO - d + tm, c]
        cnt = jnp.minimum(tpos + 1, w).astype(F32)
        pg = (tot / cnt - cur).astype(BF16)
        pooled.append(jnp.dot(pg, cw_ref[gi], preferred_element_type=F32))
    oc = jnp.concatenate(pooled, axis=1) * cs_ref[...]

    def gated(o, gate_ref, k):
        u = gate_ref[...].astype(F32)
        return jnp.dot((o * (u + u * jnp.tanh(u))).astype(BF16), wb_ref[k],
                       preferred_element_type=F32)

    def merge_gate(mg_ref, y):
        return y + jnp.tanh(mg_ref[...].astype(F32)) * y

    merged = (merge_gate(mg0_ref, gated(oa_ref[...], ag_ref, 0))
              + merge_gate(mg1_ref, gated(ob_ref[...], bg_ref, 1))
              + merge_gate(mg2_ref, gated(oc, cg_ref, 2)))
    o_ref[...] = x_ref[...] + jnp.dot(merged.astype(BF16), wo_ref[...], preferred_element_type=F32)


def _merge(x2d, z, oa, ob, cw, cs, wb, wo, *, seq):
    rows, d = x2d.shape
    tm = min(512, seq)
    bw = BRANCH_WIDTH
    kern = functools.partial(_merge_kernel, tm=tm, seq=seq)
    zcol = lambda blk: pl.BlockSpec((tm, bw), lambda i: (i, blk * LANES // bw))
    mgs = lambda k: pl.BlockSpec((tm, d), lambda i: (i, BLK_MG * LANES // d + k))
    full = lambda a: pl.BlockSpec(a.shape, lambda i: (0,) * a.ndim)
    return pl.pallas_call(
        kern,
        grid=(rows // tm,),
        in_specs=[pl.BlockSpec((tm, d), lambda i: (i, 0)),
                  pl.BlockSpec((tm, bw), lambda i: (i, 0)),
                  pl.BlockSpec((tm, bw), lambda i: (i, 0)),
                  zcol(BLK_AG), zcol(BLK_BG), zcol(BLK_CU),
                  pl.BlockSpec((HALO, bw),
                               lambda i: (jnp.maximum(i * (tm // HALO) - 1, 0), BLK_CU * LANES // bw)),
                  zcol(BLK_CG), mgs(0), mgs(1), mgs(2),
                  full(cw), full(cs), full(wb), full(wo)],
        out_specs=pl.BlockSpec((tm, d), lambda i: (i, 0)),
        out_shape=jax.ShapeDtypeStruct((rows, d), F32),
        scratch_shapes=[pltpu.VMEM((tm + HALO, bw), F32)],
        compiler_params=pltpu.CompilerParams(
            dimension_semantics=("parallel",), vmem_limit_bytes=VMEM_LIMIT),
        name="merge",
    )(x2d, oa, ob, z, z, z, z, z, z, z, z, cw, cs, wb, wo)


def _overlap_t(seq):
    ncp = seq // CMP_STRIDE
    nc = (seq - CMP_LEN) // CMP_STRIDE + 1
    ns = seq // SLC_LEN
    cs = np.arange(ncp) * CMP_STRIDE
    ce = cs + CMP_LEN
    ss = np.arange(ns) * SLC_LEN
    se = ss + SLC_LEN
    ov = np.clip(np.minimum(ce[None, :], se[:, None]) - np.maximum(cs[None, :], ss[:, None]), 0, None)
    ov = ov / CMP_LEN
    ov[:, nc:] = 0.0
    return jnp.asarray(ov, dtype=BF16)


def kernel(x, norm_g, w_in, a_q_g, a_k_g, a_lam, a_subln_g, b_q_g, b_k_g, b_cmp_pos, b_phi_w1,
           b_phi_w2, c_w, c_scale, w_branch, w_out):
    batch, seq, d = x.shape
    x2d = x.reshape(batch * seq, d)
    ovt = _overlap_t(seq)
    two = lambda v: jnp.concatenate([v, v]).reshape(1, LANES).astype(F32)
    for l in range(DEPTH):
        z = _inproj(x2d, norm_g[l].reshape(1, d), _permute_w_in(w_in[l].T.astype(BF16)))
        lam_init = 0.8 - 0.6 * math.exp(-0.3 * l)
        oa = _diff_attn(z, a_lam[l].astype(F32), two(a_q_g[l]), two(a_k_g[l]),
                        a_subln_g[l].reshape(1, LANES).astype(F32),
                        batch=batch, seq=seq, lam_init=lam_init)
        kc, vc = _compress(z, *_compress_weights(b_cmp_pos[l], b_phi_w1[l], b_phi_w2[l]),
                           two(b_k_g[l, 0]), batch=batch, seq=seq)
        ob = _nsa(z, kc, vc, ovt, two(b_q_g[l]), two(b_k_g[l, 1]), two(b_k_g[l, 2]),
                  batch=batch, seq=seq)
        x2d = _merge(x2d, z, oa, ob, c_w[l].astype(BF16), c_scale[l].reshape(1, BRANCH_WIDTH),
                     (w_branch[l] * 0.5).astype(BF16), w_out[l].astype(BF16), seq=seq)
    return x2d.reshape(batch, seq, d)
```

```python
import functools
import math

import numpy as np
import jax
import jax.numpy as jnp
from jax import lax
from jax.experimental import pallas as pl
from jax.experimental.pallas import tpu as pltpu

F32 = jnp.float32
BF16 = jnp.bfloat16

D_MODEL = 1024
DEPTH = 2
BRANCH_WIDTH = D_MODEL // 2
A_DK = 64
A_DV = 2 * A_DK
A_HEADS = BRANCH_WIDTH // A_DV
B_DK = 64
B_HEADS = BRANCH_WIDTH // B_DK
B_GROUPS = 2
B_HPG = B_HEADS // B_GROUPS
CMP_LEN = 32
CMP_STRIDE = 16
SLC_LEN = 64
SLC_TOPK = 16
WIN = 512
PHI_HIDDEN = 256
N_NSA_BRANCH = 3
C_WINDOWS = (2, 4, 8, 16)
C_GDIM = BRANCH_WIDTH // len(C_WINDOWS)
N_BRANCH = 3
EPS = 1e-6
NEG_INF = -1e30
BIG = 1e30

LANES = 128
HALO = 16
ONES_ROWS = 16
PREP_ROWS = 2048
VMEM_LIMIT = 48 * 1024 * 1024

ZW = 8192
BLK_AQ, BLK_AK, BLK_AV, BLK_AG = 0, 4, 8, 12
BLK_BQ, BLK_BG, BLK_CU, BLK_CG = 16, 20, 24, 28
BLK_MG = 32
BLK_BKV = 56
BLK_BBG = 62

SCALE = 0.125
QSCALE = SCALE * math.log2(math.e)


def _permute_w_in(wt):
    o = np.cumsum([0, 512, 512, 512, 512, 512, 768, 24, 512, 512, 512, 3072])
    seg = lambda i: wt[int(o[i]):int(o[i + 1])]
    aq, ak, av, ag, bq, bkv, bbg, bg, cu, cg, mg = [seg(i) for i in range(11)]
    parts = [aq, ak, av, ag * 0.5, bq, bg * 0.5, cu, cg * 0.5, mg * 0.5, bkv, bbg]
    used = sum(p.shape[0] for p in parts)
    parts.append(jnp.zeros((ZW - used, wt.shape[1]), wt.dtype))
    return jnp.concatenate(parts, axis=0)


def _sigmoid(x):
    return 0.5 * jnp.tanh(0.5 * x) + 0.5


def _half_norm(x, gain, lo):
    x2 = x * x
    s_lo = jnp.sum(jnp.where(lo, x2, 0.0), axis=-1, keepdims=True)
    s_hi = jnp.sum(jnp.where(lo, 0.0, x2), axis=-1, keepdims=True)
    ms = jnp.where(lo, s_lo, s_hi) * (1.0 / 64.0)
    return x * lax.rsqrt(ms + EPS) * gain


def _inproj_kernel(x_ref, g_ref, w_ref, o_ref, h_ref):
    @pl.when(pl.program_id(1) == 0)
    def _():
        x = x_ref[...]
        ms = jnp.mean(x * x, axis=-1, keepdims=True)
        h_ref[...] = (x * lax.rsqrt(ms + EPS) * g_ref[...]).astype(BF16)

    o_ref[...] = lax.dot_general(h_ref[...], w_ref[...], (((1,), (1,)), ((), ())),
                                 preferred_element_type=F32).astype(o_ref.dtype)


def _inproj(x2d, g, wt):
    rows, d = x2d.shape
    tm, tn = min(1024, rows), 2048
    return pl.pallas_call(
        _inproj_kernel,
        grid=(rows // tm, ZW // tn),
        in_specs=[pl.BlockSpec((tm, d), lambda i, j: (i, 0)),
                  pl.BlockSpec((1, d), lambda i, j: (0, 0)),
                  pl.BlockSpec((tn, d), lambda i, j: (j, 0))],
        out_specs=pl.BlockSpec((tm, tn), lambda i, j: (i, j)),
        out_shape=jax.ShapeDtypeStruct((rows, ZW), BF16),
        scratch_shapes=[pltpu.VMEM((tm, d), BF16)],
        compiler_params=pltpu.CompilerParams(
            dimension_semantics=("parallel", "arbitrary"), vmem_limit_bytes=VMEM_LIMIT),
        name="inproj",
    )(x2d, g, wt)


def _reset(m_ref, l_ref, acc_ref):
    m_ref[...] = jnp.full(m_ref.shape, NEG_INF, F32)
    l_ref[...] = jnp.zeros(l_ref.shape, F32)
    acc_ref[...] = jnp.zeros(acc_ref.shape, F32)


def _softmax_step_t(s, vt_tile, m_ref, l_ref, acc_ref):
    m_old = m_ref[...]
    m_new = jnp.maximum(m_old, jnp.max(s, axis=0, keepdims=True))
    alpha = jnp.exp2(m_old - m_new)
    p = jnp.exp2(s - m_new)
    l_ref[...] = alpha * l_ref[...] + jnp.sum(p, axis=0, keepdims=True)
    acc_ref[...] = alpha * acc_ref[...] + jnp.dot(vt_tile, p.astype(BF16),
                                                  preferred_element_type=F32)
    m_ref[...] = m_new


def _causal_tiles_t(nfull, chains):
    def issue(j, into_b):
        for scores_fn, _, _, sa_ref, sb_ref, _ in chains:
            (sb_ref if into_b else sa_ref)[...] = scores_fn(j)

    def consume(j, from_b, masked=False):
        for _, value_fn, mask_fn, sa_ref, sb_ref, state in chains:
            s = (sb_ref if from_b else sa_ref)[...]
            _softmax_step_t(mask_fn(s) if masked else s, value_fn(j), *state)

    def two_tiles(j):
        issue(j + 1, True)
        consume(j, False)
        issue(j + 2, False)
        consume(j + 1, True)

    issue(0, False)
    done = 0
    for per_trip in (8, 4, 2):
        def trip(i, carry, per_trip=per_trip, done=done):
            for t in range(0, per_trip, 2):
                two_tiles(done + per_trip * i + t)
            return carry
        ntrip = (nfull - done) // per_trip
        lax.fori_loop(0, ntrip, trip, 0)
        done = done + per_trip * ntrip
    odd = (nfull & 1) == 1

    @pl.when(odd)
    def _():
        issue(nfull, True)
        consume(nfull - 1, False)
        consume(nfull, True, masked=True)

    @pl.when(jnp.logical_not(odd))
    def _():
        consume(nfull, False, masked=True)


def _transpose_values(v_ref, vt_ref, seq):
    chunk = min(PREP_ROWS, seq)

    def body(c, carry):
        r = pl.ds(pl.multiple_of(c * chunk, chunk), chunk)
        vt_ref[:, r] = v_ref[r, :].astype(F32).T.astype(BF16)
        return carry
    lax.fori_loop(0, seq // chunk, body, 0)


def _normalize_keys(src_ref, dst_ref, gain, lo, seq):
    chunk = min(PREP_ROWS, seq)

    def body(c, carry):
        r = pl.ds(pl.multiple_of(c * chunk, chunk), chunk)
        dst_ref[r, :] = _half_norm(src_ref[r, :].astype(F32), gain, lo).astype(BF16)
        return carry
    lax.fori_loop(0, seq // chunk, body, 0)


A_HEADS_PER_STEP = 2


def _diff_attn_kernel(lam_ref, q_ref, k_ref, v_ref, qg_ref, kg_ref, sg_ref, o_ref, *scratch,
                      tq, tk, seq, lam_init):
    qi = pl.program_id(2)
    lo = lax.broadcasted_iota(jnp.int32, (1, LANES), 1) < 64
    per_head = [scratch[7 * h:7 * h + 7] for h in range(A_HEADS_PER_STEP)]

    @pl.when(qi == 0)
    def _():
        for h, (kn_ref, vt_ref, *_) in enumerate(per_head):
            lanes = slice(h * LANES, (h + 1) * LANES)
            _normalize_keys(k_ref.at[:, lanes], kn_ref, kg_ref[...], lo, seq)
            _transpose_values(v_ref.at[:, lanes], vt_ref, seq)

    q0 = qi * tq
    nfull = q0 // tk
    top = lax.broadcasted_iota(jnp.int32, (LANES, 1), 0) < 64

    def rows_of(j):
        return pl.ds(pl.multiple_of(j * tk, tk), tk)

    def causal(s):
        qpos = q0 + lax.broadcasted_iota(jnp.int32, (tk, tq), 1)
        kpos = nfull * tk + lax.broadcasted_iota(jnp.int32, (tk, tq), 0)
        bias = jnp.where(kpos <= qpos, 0.0, NEG_INF)
        return s + jnp.concatenate([bias, bias], axis=1)

    def chain(h):
        kn_ref, vt_ref, sa_ref, sb_ref, m_ref, l_ref, acc_ref = per_head[h]
        q = q_ref[:, h * LANES:(h + 1) * LANES].astype(F32)
        qt = (_half_norm(q, qg_ref[...], lo) * QSCALE).T
        qq = jnp.concatenate([jnp.where(top, qt, 0.0), jnp.where(top, 0.0, qt)],
                             axis=1).astype(BF16)
        _reset(m_ref, l_ref, acc_ref)
        return (lambda j: jnp.dot(kn_ref[rows_of(j), :], qq, preferred_element_type=F32),
                lambda j: vt_ref[:, rows_of(j)],
                causal, sa_ref, sb_ref, (m_ref, l_ref, acc_ref))

    _causal_tiles_t(nfull, [chain(h) for h in range(A_HEADS_PER_STEP)])

    lp = lam_ref[...]
    lam = (jnp.exp(jnp.sum(lp[0:1] * lp[1:2], axis=-1, keepdims=True))
           - jnp.exp(jnp.sum(lp[2:3] * lp[3:4], axis=-1, keepdims=True)) + lam_init)
    for h, (_, _, _, _, _, l_ref, acc_ref) in enumerate(per_head):
        acc = acc_ref[...]
        l = l_ref[...]
        o = acc[:, :tq] / l[:, :tq] - lam * (acc[:, tq:] / l[:, tq:])
        ms = jnp.mean(o * o, axis=0, keepdims=True)
        o_ref[:, h * LANES:(h + 1) * LANES] = ((o * lax.rsqrt(ms + EPS)).T * sg_ref[...]
                                               * (1.0 - lam_init))


def _diff_attn(z, lam_p, qg, kg, sg, *, batch, seq, lam_init):
    tq = min(512, seq)
    tk = min(512, seq)
    nq = seq // tq
    hps = A_HEADS_PER_STEP
    wide = hps * LANES
    kern = functools.partial(_diff_attn_kernel, tq=tq, tk=tk, seq=seq, lam_init=lam_init)
    vec = lambda: pl.BlockSpec((1, LANES), lambda b, h, i: (0, 0))
    head_scratch = [pltpu.VMEM((seq, LANES), BF16),
                    pltpu.VMEM((LANES, seq), BF16),
                    pltpu.VMEM((tk, 2 * tq), F32),
                    pltpu.VMEM((tk, 2 * tq), F32),
                    pltpu.VMEM((1, 2 * tq), F32),
                    pltpu.VMEM((1, 2 * tq), F32),
                    pltpu.VMEM((LANES, 2 * tq), F32)]
    return pl.pallas_call(
        kern,
        grid=(batch, A_HEADS // hps, nq),
        in_specs=[pl.BlockSpec((4, A_DK), lambda b, h, i: (0, 0)),
                  pl.BlockSpec((tq, wide), lambda b, h, i: (b * nq + i, BLK_AQ // hps + h)),
                  pl.BlockSpec((seq, wide), lambda b, h, i: (b, BLK_AK // hps + h)),
                  pl.BlockSpec((seq, wide), lambda b, h, i: (b, BLK_AV // hps + h)),
                  vec(), vec(), vec()],
        out_specs=pl.BlockSpec((tq, wide), lambda b, h, i: (b * nq + i, h)),
        out_shape=jax.ShapeDtypeStruct((batch * seq, BRANCH_WIDTH), F32),
        scratch_shapes=head_scratch * hps,
        compiler_params=pltpu.CompilerParams(
            dimension_semantics=("parallel", "parallel", "arbitrary"),
            vmem_limit_bytes=VMEM_LIMIT),
        name="diff_attn",
    )(lam_p, z, z, z, qg, kg, sg)


def _compress_kernel(z_ref, pos_ref, w1_ref, w2_ref, g_ref, kc_ref, vct_ref, zf_ref, x_ref,
                     *, seq):
    t_kv = pl.program_id(1)
    ncp = seq // CMP_STRIDE
    lo = lax.broadcasted_iota(jnp.int32, (1, LANES), 1) < 64
    zf_ref[0:seq, :] = z_ref[...].astype(F32)
    zf_ref[seq:, :] = jnp.zeros((CMP_STRIDE, LANES), F32)
    for t in range(CMP_LEN):
        rows = zf_ref[pl.ds(t, ncp, stride=CMP_STRIDE), :]
        x_ref[:, t * LANES:(t + 1) * LANES] = (rows + pos_ref[t:t + 1, :]).astype(BF16)
    h = jnp.dot(x_ref[...], w1_ref[...], preferred_element_type=F32)
    h = h * _sigmoid(h)
    y = jnp.dot(h.astype(BF16), w2_ref[...], preferred_element_type=F32)

    @pl.when(t_kv == 0)
    def _():
        kc_ref[...] = _half_norm(y, g_ref[...], lo).astype(BF16)

    @pl.when(t_kv != 0)
    def _():
        vct_ref[0:LANES, :] = y.T.astype(BF16)
        vct_ref[LANES:, :] = jnp.ones((ONES_ROWS, ncp), BF16)


def _compress(z, pos, w1, w2, gain, *, batch, seq):
    ncp = seq // CMP_STRIDE
    wide = CMP_LEN * LANES
    return pl.pallas_call(
        functools.partial(_compress_kernel, seq=seq),
        grid=(batch, 2),
        in_specs=[pl.BlockSpec((seq, LANES), lambda b, t: (b, BLK_BKV + t)),
                  pl.BlockSpec((None, CMP_LEN, LANES), lambda b, t: (t, 0, 0)),
                  pl.BlockSpec((None, wide, 2 * PHI_HIDDEN), lambda b, t: (t, 0, 0)),
                  pl.BlockSpec((None, 2 * PHI_HIDDEN, LANES), lambda b, t: (t, 0, 0)),
                  pl.BlockSpec((1, LANES), lambda b, t: (0, 0))],
        out_specs=[pl.BlockSpec((None, ncp, LANES), lambda b, t: (b, 0, 0)),
                   pl.BlockSpec((None, LANES + ONES_ROWS, ncp), lambda b, t: (b, 0, 0))],
        out_shape=[jax.ShapeDtypeStruct((batch, ncp, LANES), BF16),
                   jax.ShapeDtypeStruct((batch, LANES + ONES_ROWS, ncp), BF16)],
        scratch_shapes=[pltpu.VMEM((seq + CMP_STRIDE, LANES), F32),
                        pltpu.VMEM((ncp, wide), BF16)],
        compiler_params=pltpu.CompilerParams(
            dimension_semantics=("arbitrary", "arbitrary"), vmem_limit_bytes=VMEM_LIMIT),
        name="compress",
    )(z, pos, w1, w2, gain)


def _compress_weights(pos, w1, w2):
    pos2 = jnp.concatenate([pos, pos], axis=-1).astype(F32)
    w1r = w1.reshape(2, CMP_LEN, B_DK, PHI_HIDDEN)
    z1 = jnp.zeros_like(w1r)
    w1b = jnp.concatenate([jnp.concatenate([w1r, z1], axis=-1),
                           jnp.concatenate([z1, w1r], axis=-1)], axis=2)
    z2 = jnp.zeros_like(w2)
    w2b = jnp.concatenate([jnp.concatenate([w2, z2], axis=-1),
                           jnp.concatenate([z2, w2], axis=-1)], axis=1)
    return pos2, w1b.reshape(2, CMP_LEN * LANES, 2 * PHI_HIDDEN).astype(BF16), w2b.astype(BF16)


def _split2_bf16(x):
    hi = x.astype(BF16)
    return hi, (x - hi.astype(F32)).astype(BF16)


def _not_selected_t(val, k):
    nb, nqry = val.shape
    kf = float(k)
    sub = lax.broadcasted_iota(jnp.int32, (8, nqry), 0)

    def over_blocks(xs, op):
        xs = list(xs)
        while len(xs) > 1:
            xs = [op(xs[i], xs[i + 1]) for i in range(0, len(xs) - 1, 2)] + xs[len(xs) & ~1:]
        acc = xs[0]
        for sh in (4, 2, 1):
            acc = op(acc, pltpu.roll(acc, sh, 0))
        return acc

    vals = [val[8 * r:8 * r + 8] for r in range(nb // 8)]
    rem = vals
    thr = jnp.full((8, nqry), -jnp.inf, F32)
    done = jnp.zeros((8, nqry), F32)
    cnt = jnp.zeros((8, nqry), F32)
    for _ in range(k):
        top = over_blocks(rem, jnp.maximum)
        hit = [x == top for x in rem]
        cnt = cnt + over_blocks([jnp.where(h, 1.0, 0.0) for h in hit], jnp.add)
        rem = [jnp.where(h, -jnp.inf, x) for h, x in zip(hit, rem)]
        thr = jnp.where(done > 0.0, thr, top)
        done = jnp.where(cnt >= kf, 1.0, done)
    above = [jnp.where(v > thr, 1.0, 0.0) for v in vals]
    need = kf - over_blocks(above, jnp.add)
    offset = jnp.zeros((8, nqry), F32)
    out = []
    for v, ab in zip(vals, above):
        tie = jnp.where(v == thr, 1.0, 0.0)
        scan = tie
        for sh in (1, 2, 4):
            scan = scan + jnp.where(sub >= sh, pltpu.roll(scan, sh, 0), 0.0)
        keep = ab + tie * jnp.where(offset + scan <= need, 1.0, 0.0)
        out.append(1.0 - keep)
        offset = offset + over_blocks([tie], jnp.add)
    return jnp.concatenate(out, axis=0)


def _nsa_kernel(q_ref, kc_ref, vct_ref, ks_ref, vs_ref, kw_ref, vw_ref, gt_ref, ovt_ref,
                qg_ref, ksg_ref, kwg_ref, o_ref,
                ksn_ref, kwn_ref, vst_ref, vwt_ref, sa_ref, sb_ref, m_ref, l_ref, acc_ref,
                part_ref, gsel_ref, *, tq, tk, seq):
    qi = pl.program_id(1)
    g = pl.program_id(2)
    ns = seq // SLC_LEN
    ncp = seq // CMP_STRIDE
    rows = B_HPG * tq
    half = LANES // 2
    lo = lax.broadcasted_iota(jnp.int32, (1, LANES), 1) < half
    g0 = (jnp.zeros((half, tq), jnp.int32) + g) == 0
    q0 = qi * tq

    @pl.when((qi == 0) & (g == 0))
    def _():
        _normalize_keys(kw_ref, kwn_ref, kwg_ref[...], lo, seq)
        _transpose_values(vs_ref, vst_ref, seq)
        _transpose_values(vw_ref, vwt_ref.at[0:LANES, :], seq)
        vwt_ref[LANES:, :] = jnp.ones((ONES_ROWS, seq), BF16)
        chunk = min(PREP_ROWS, seq)

        def body(c, carry):
            r = pl.ds(pl.multiple_of(c * chunk, chunk), chunk)
            ksn_ref[r, 0:LANES] = _half_norm(ks_ref[r, :].astype(F32), ksg_ref[...], lo).astype(BF16)
            kblk = (c * chunk + lax.broadcasted_iota(jnp.int32, (chunk, LANES), 0)) // SLC_LEN
            own = lax.broadcasted_iota(jnp.int32, (chunk, LANES), 1) == kblk
            ksn_ref[r, LANES:2 * LANES] = jnp.where(own, NEG_INF, 0.0).astype(BF16)
            return carry
        lax.fori_loop(0, seq // chunk, body, 0)

    zeros = jnp.zeros((half, tq), F32)
    heads = []
    for pair in range(B_HPG // 2):
        xt = (_half_norm(q_ref[:, pair * LANES:(pair + 1) * LANES].astype(F32), qg_ref[...], lo)
              * QSCALE).T
        for hh in range(2):
            hrows = xt[hh * half:(hh + 1) * half]
            heads.append(jnp.concatenate([jnp.where(g0, hrows, zeros),
                                          jnp.where(g0, zeros, hrows)], axis=0))
    qs = jnp.concatenate(heads, axis=1).astype(BF16)
    qpos = q0 + (lax.broadcasted_iota(jnp.int32, (1, rows), 1) & (tq - 1))
    qpos1 = q0 + lax.broadcasted_iota(jnp.int32, (1, tq), 1)

    def per_head(bias):
        return jnp.concatenate([bias] * B_HPG, axis=1)

    gates = _sigmoid(gt_ref[...].astype(F32)).T
    g0row = (jnp.zeros((1, tq), jnp.int32) + g) == 0

    def gate_row(br):
        per_head = []
        for h in range(B_HPG):
            c = h * N_NSA_BRANCH + br
            c1 = B_HPG * N_NSA_BRANCH + c
            per_head.append(jnp.where(g0row, gates[c:c + 1], gates[c1:c1 + 1]))
        return jnp.concatenate(per_head, axis=1)

    s = jnp.dot(kc_ref[...], qs, preferred_element_type=F32)
    cmp_end = lax.broadcasted_iota(jnp.int32, (ncp, 1), 0) * CMP_STRIDE + (CMP_LEN - 1)
    s = s + per_head(jnp.where(cmp_end <= qpos1, 0.0, NEG_INF))
    e = jnp.exp2(s - jnp.max(s, axis=0, keepdims=True))
    res = jnp.dot(vct_ref[...], e.astype(BF16), preferred_element_type=F32)
    inv = jnp.where(qpos >= CMP_LEN - 1, 1.0 / res[LANES:LANES + 1], 0.0)
    p_cmp = e * inv
    o_cmp = res[0:LANES] * inv

    nk = WIN + tq
    start = pl.multiple_of(jnp.maximum(q0 - WIN, 0), tq)
    kpos = start + lax.broadcasted_iota(jnp.int32, (nk, 1), 0)
    sw = jnp.dot(kwn_ref[pl.ds(start, nk), :], qs, preferred_element_type=F32)
    sw = sw + per_head(jnp.where((kpos <= qpos1) & (kpos > qpos1 - WIN), 0.0, NEG_INF))
    pw = jnp.exp2(sw - jnp.max(sw, axis=0, keepdims=True))
    res = jnp.dot(vwt_ref[:, pl.ds(start, nk)], pw.astype(BF16), preferred_element_type=F32)
    o_win = res[0:LANES] / res[LANES:LANES + 1]

    part_ref[...] = gate_row(0) * o_cmp + gate_row(2) * o_win
    gsel_ref[...] = gate_row(1)

    psum = p_cmp[:, 0:tq]
    for h in range(1, B_HPG):
        psum = psum + p_cmp[:, h * tq:(h + 1) * tq]
    ovt = ovt_ref[...]
    imp_t = sum(jnp.dot(ovt, part, preferred_element_type=F32) for part in _split2_bf16(psum))
    blk = lax.broadcasted_iota(jnp.int32, (ns, tq), 0)
    cur = (q0 + lax.broadcasted_iota(jnp.int32, (ns, tq), 1)) // SLC_LEN
    val = jnp.where((blk == 0) | (blk == cur) | (blk == cur - 1), BIG, imp_t)
    val = jnp.where(blk > cur, NEG_INF, val)
    notsel = _not_selected_t(val, min(SLC_TOPK, ns))
    if ns < LANES:
        notsel = jnp.concatenate([notsel, jnp.zeros((LANES - ns, tq), F32)], axis=0)
    qa = jnp.concatenate([qs, jnp.concatenate([notsel.astype(BF16)] * B_HPG, axis=1)], axis=0)

    _reset(m_ref, l_ref, acc_ref)
    nfull = q0 // tk

    def rows_of(j):
        return pl.ds(pl.multiple_of(j * tk, tk), tk)

    def causal(sc):
        kpos = nfull * tk + lax.broadcasted_iota(jnp.int32, (tk, 1), 0)
        return sc + per_head(jnp.where(kpos <= qpos1, 0.0, NEG_INF))

    _causal_tiles_t(nfull, [(
        lambda j: jnp.dot(ksn_ref[rows_of(j), :], qa, preferred_element_type=F32),
        lambda j: vst_ref[:, rows_of(j)],
        causal, sa_ref, sb_ref, (m_ref, l_ref, acc_ref))])
    o = part_ref[...] + gsel_ref[...] * (acc_ref[...] / l_ref[...])
    out_t = jnp.concatenate(
        [jnp.where(g0, o[0:half, h * tq:(h + 1) * tq], o[half:LANES, h * tq:(h + 1) * tq])
         for h in range(B_HPG)], axis=0)
    o_ref[...] = out_t.T


def _nsa(z, kc, vc, ovt, qg, ksg, kwg, *, batch, seq):
    tq = min(256, seq)
    tk = min(512, seq)
    nq = seq // tq
    ns = seq // SLC_LEN
    ncp = seq // CMP_STRIDE
    rows = B_HPG * tq
    kern = functools.partial(_nsa_kernel, tq=tq, tk=tk, seq=seq)
    kv = lambda blk: pl.BlockSpec((seq, LANES), lambda b, i, g: (b, blk))
    vec = lambda: pl.BlockSpec((1, LANES), lambda b, i, g: (0, 0))
    assert seq >= WIN + tq and WIN % tq == 0 and tk % tq == 0
    return pl.pallas_call(
        kern,
        grid=(batch, nq, B_GROUPS),
        in_specs=[pl.BlockSpec((tq, 2 * LANES), lambda b, i, g: (b * nq + i, BLK_BQ // 2 + g)),
                  pl.BlockSpec((None, ncp, LANES), lambda b, i, g: (b, 0, 0)),
                  pl.BlockSpec((None, LANES + ONES_ROWS, ncp), lambda b, i, g: (b, 0, 0)),
                  kv(BLK_BKV + 2), kv(BLK_BKV + 3), kv(BLK_BKV + 4), kv(BLK_BKV + 5),
                  pl.BlockSpec((tq, LANES), lambda b, i, g: (b * nq + i, BLK_BBG)),
                  pl.BlockSpec((ns, ncp), lambda b, i, g: (0, 0)),
                  vec(), vec(), vec()],
        out_specs=pl.BlockSpec((tq, 2 * LANES), lambda b, i, g: (b * nq + i, g)),
        out_shape=jax.ShapeDtypeStruct((batch * seq, BRANCH_WIDTH), F32),
        scratch_shapes=[pltpu.VMEM((seq, 2 * LANES), BF16),
                        pltpu.VMEM((seq, LANES), BF16),
                        pltpu.VMEM((LANES, seq), BF16),
                        pltpu.VMEM((LANES + ONES_ROWS, seq), BF16),
                        pltpu.VMEM((tk, rows), F32),
                        pltpu.VMEM((tk, rows), F32),
                        pltpu.VMEM((1, rows), F32),
                        pltpu.VMEM((1, rows), F32),
                        pltpu.VMEM((LANES, rows), F32),
                        pltpu.VMEM((LANES, rows), F32),
                        pltpu.VMEM((1, rows), F32)],
        compiler_params=pltpu.CompilerParams(
            dimension_semantics=("parallel", "arbitrary", "arbitrary"),
            vmem_limit_bytes=VMEM_LIMIT),
        name="nsa",
    )(z, kc, vc, z, z, z, z, z, ovt, qg, ksg, kwg)


def _merge_kernel(x_ref, oa_ref, ob_ref, ag_ref, bg_ref, cu_ref, halo_ref, cg_ref,
                  mg0_ref, mg1_ref, mg2_ref, cw_ref, cs_ref, wb_ref, wo_ref, o_ref,
                  ext_ref, *, tm, seq):
    i = pl.program_id(0)
    tpos0 = (i * tm) % seq

    ext_ref[HALO:, :] = cu_ref[...].astype(F32)
    ext_ref[:HALO, :] = jnp.where(tpos0 == 0, 0.0, halo_ref[...].astype(F32))
    tpos = tpos0 + lax.broadcasted_iota(jnp.int32, (tm, 1), 0)
    pooled = []
    for gi, w in enumerate(C_WINDOWS):
        c = slice(gi * C_GDIM, (gi + 1) * C_GDIM)
        cur = ext_ref[HALO:, c]
        tot = cur
        for d in range(1, w):
            tot = tot + ext_ref[HALO - d:HALO - d + tm, c]
        cnt = jnp.minimum(tpos + 1, w).astype(F32)
        pg = (tot / cnt - cur).astype(BF16)
        pooled.append(jnp.dot(pg, cw_ref[gi], preferred_element_type=F32))
    oc = jnp.concatenate(pooled, axis=1) * cs_ref[...]

    def gated(o, gate_ref, k):
        u = gate_ref[...].astype(F32)
        return jnp.dot((o * (u + u * jnp.tanh(u))).astype(BF16), wb_ref[k],
                       preferred_element_type=F32)

    def merge_gate(mg_ref, y):
        return y + jnp.tanh(mg_ref[...].astype(F32)) * y

    merged = (merge_gate(mg0_ref, gated(oa_ref[...], ag_ref, 0))
              + merge_gate(mg1_ref, gated(ob_ref[...], bg_ref, 1))
              + merge_gate(mg2_ref, gated(oc, cg_ref, 2)))
    o_ref[...] = x_ref[...] + jnp.dot(merged.astype(BF16), wo_ref[...], preferred_element_type=F32)


def _merge(x2d, z, oa, ob, cw, cs, wb, wo, *, seq):
    rows, d = x2d.shape
    tm = min(512, seq)
    bw = BRANCH_WIDTH
    kern = functools.partial(_merge_kernel, tm=tm, seq=seq)
    zcol = lambda blk: pl.BlockSpec((tm, bw), lambda i: (i, blk * LANES // bw))
    mgs = lambda k: pl.BlockSpec((tm, d), lambda i: (i, BLK_MG * LANES // d + k))
    full = lambda a: pl.BlockSpec(a.shape, lambda i: (0,) * a.ndim)
    return pl.pallas_call(
        kern,
        grid=(rows // tm,),
        in_specs=[pl.BlockSpec((tm, d), lambda i: (i, 0)),
                  pl.BlockSpec((tm, bw), lambda i: (i, 0)),
                  pl.BlockSpec((tm, bw), lambda i: (i, 0)),
                  zcol(BLK_AG), zcol(BLK_BG), zcol(BLK_CU),
                  pl.BlockSpec((HALO, bw),
                               lambda i: (jnp.maximum(i * (tm // HALO) - 1, 0), BLK_CU * LANES // bw)),
                  zcol(BLK_CG), mgs(0), mgs(1), mgs(2),
                  full(cw), full(cs), full(wb), full(wo)],
        out_specs=pl.BlockSpec((tm, d), lambda i: (i, 0)),
        out_shape=jax.ShapeDtypeStruct((rows, d), F32),
        scratch_shapes=[pltpu.VMEM((tm + HALO, bw), F32)],
        compiler_params=pltpu.CompilerParams(
            dimension_semantics=("parallel",), vmem_limit_bytes=VMEM_LIMIT),
        name="merge",
    )(x2d, oa, ob, z, z, z, z, z, z, z, z, cw, cs, wb, wo)


def _overlap_t(seq):
    ncp = seq // CMP_STRIDE
    nc = (seq - CMP_LEN) // CMP_STRIDE + 1
    ns = seq // SLC_LEN
    cs = np.arange(ncp) * CMP_STRIDE
    ce = cs + CMP_LEN
    ss = np.arange(ns) * SLC_LEN
    se = ss + SLC_LEN
    ov = np.clip(np.minimum(ce[None, :], se[:, None]) - np.maximum(cs[None, :], ss[:, None]), 0, None)
    ov = ov / CMP_LEN
    ov[:, nc:] = 0.0
    return jnp.asarray(ov, dtype=BF16)


def kernel(x, norm_g, w_in, a_q_g, a_k_g, a_lam, a_subln_g, b_q_g, b_k_g, b_cmp_pos, b_phi_w1,
           b_phi_w2, c_w, c_scale, w_branch, w_out):
    batch, seq, d = x.shape
    x2d = x.reshape(batch * seq, d)
    ovt = _overlap_t(seq)
    two = lambda v: jnp.concatenate([v, v]).reshape(1, LANES).astype(F32)
    for l in range(DEPTH):
        z = _inproj(x2d, norm_g[l].reshape(1, d), _permute_w_in(w_in[l].T.astype(BF16)))
        lam_init = 0.8 - 0.6 * math.exp(-0.3 * l)
        oa = _diff_attn(z, a_lam[l].astype(F32), two(a_q_g[l]), two(a_k_g[l]),
                        a_subln_g[l].reshape(1, LANES).astype(F32),
                        batch=batch, seq=seq, lam_init=lam_init)
        kc, vc = _compress(z, *_compress_weights(b_cmp_pos[l], b_phi_w1[l], b_phi_w2[l]),
                           two(b_k_g[l, 0]), batch=batch, seq=seq)
        ob = _nsa(z, kc, vc, ovt, two(b_q_g[l]), two(b_k_g[l, 1]), two(b_k_g[l, 2]),
                  batch=batch, seq=seq)
        x2d = _merge(x2d, z, oa, ob, c_w[l].astype(BF16), c_scale[l].reshape(1, BRANCH_WIDTH),
                     (w_branch[l] * 0.5).astype(BF16), w_out[l].astype(BF16), seq=seq)
    return x2d.reshape(batch, seq, d)
```
